```python
import jax, jax.numpy as jnp
from jax import lax
import numpy as np

D_MODEL = 1024
BATCH = 4
SEQ = 8192
DEPTH = 2

N_EVEN = (DEPTH + 1) // 2
N_ODD = DEPTH // 2
CHUNK = 128
CONV_K = 4
NORM_EPS = 1e-6
ROPE_BASE = 10000.0

RET_HEADS = 4
RET_DIM = 128
RET_WIDTH = RET_HEADS * RET_DIM
ML_HEADS = 4
ML_DIM = 128
ML_WIDTH = ML_HEADS * ML_DIM
AB_IN = 4 * RET_WIDTH + 4 * ML_WIDTH + 2 * ML_HEADS
AB_MIX = RET_WIDTH + ML_WIDTH

DN_HEADS = 8
DN_DIM = 128
DN_WIDTH = DN_HEADS * DN_DIM
DN_IN = 4 * DN_WIDTH + 2 * DN_HEADS

PEER_HEADS = 8
PEER_NK = 128
PEER_EXPERTS = PEER_NK * PEER_NK
PEER_TOPK = 16
PEER_QDIM = 256
PEER_HALF = PEER_QDIM // 2
PEER_BLOCK = 128
PEER_V_SCALE = 0.1

PLE_DIM = 256

kernel_name = "hybrid_retention_mlstm_gdn_peer"

F32 = jnp.float32


def rmsnorm(x, w):
    xf = x.astype(F32)
    y = xf * lax.rsqrt(jnp.mean(xf * xf, axis=-1, keepdims=True) + NORM_EPS)
    return (y * w.astype(F32)).astype(x.dtype)


def head_layernorm(y, w):
    mu = jnp.mean(y, axis=-1, keepdims=True)
    yc = y - mu
    yn = yc * lax.rsqrt(jnp.mean(yc * yc, axis=-1, keepdims=True) + NORM_EPS)
    B, S, H, d = y.shape
    return yn.reshape(B, S, H * d) * w.astype(F32)


def l2norm(x):
    return x * lax.rsqrt(jnp.sum(x * x, axis=-1, keepdims=True) + NORM_EPS)


def split_heads(x, n_heads):
    B, S, W = x.shape
    return x.reshape(B, S, n_heads, W // n_heads)


def to_chunks(x):
    B, S, H = x.shape[:3]
    y = x.reshape((B, S // CHUNK, CHUNK, H) + x.shape[3:])
    return y.transpose((1, 0, 3, 2) + tuple(range(4, y.ndim)))


def from_chunks(y):
    n, B, H, C, d = y.shape
    return y.transpose(1, 0, 3, 2, 4).reshape(B, n * C, H, d)


def causal_conv(x, w):
    K, ch = w.shape
    return lax.conv_general_dilated(
        x, w.astype(x.dtype)[:, None, :], window_strides=(1,), padding=[(K - 1, 0)],
        dimension_numbers=("NWC", "WIO", "NWC"), feature_group_count=ch)


def rotary(x):
    S, d = x.shape[1], x.shape[3]
    pos = jnp.arange(S, dtype=F32)
    inv = 1.0 / (ROPE_BASE ** (jnp.arange(0, d, 2, dtype=F32) / d))
    ang = pos[:, None] * inv[None, :]
    cos = jnp.cos(ang)[None, :, None, :]
    sin = jnp.sin(ang)[None, :, None, :]
    x1, x2 = x[..., : d // 2], x[..., d // 2:]
    return jnp.concatenate([x1 * cos - x2 * sin, x2 * cos + x1 * sin], axis=-1)


def retention_chunkwise(q, k, v):
    B, S, H, d = q.shape
    dv = v.shape[-1]
    log_g = jnp.log(1.0 - 2.0 ** (-5.0 - jnp.arange(H, dtype=F32)))
    idx = jnp.arange(CHUNK, dtype=F32)
    diff = idx[:, None] - idx[None, :]
    causal = diff >= 0
    dmat = jnp.where(causal[None], jnp.exp(jnp.where(causal, diff, 0.0)[None] * log_g[:, None, None]), 0.0)
    xi = jnp.exp((idx + 1.0)[None, :] * log_g[:, None])
    zeta = jnp.exp((CHUNK - 1.0 - idx)[None, :] * log_g[:, None])
    g_chunk = jnp.exp(CHUNK * log_g)
    qc, kc, vc = to_chunks(q), to_chunks(k * d ** -0.5), to_chunks(v)

    def step(state, inp):
        qb, kb, vb = inp
        s = jnp.einsum("bhid,bhjd->bhij", qb, kb) * dmat
        o = jnp.einsum("bhij,bhje->bhie", s, vb) + jnp.einsum("bhid,bhde->bhie", qb, state) * xi[:, :, None]
        state = g_chunk[:, None, None] * state + jnp.einsum("bhjd,bhje->bhde", kb * zeta[:, :, None], vb)
        return state, o

    init = jnp.zeros((B, H, d, dv), F32)
    _, o = lax.scan(step, init, (qc, kc, vc))
    return from_chunks(o)


def mlstm_chunkwise(q, k, v, i_pre, f_pre):
    B, S, H, d = q.shape
    dv = v.shape[-1]
    qc, kc, vc = to_chunks(q), to_chunks(k * d ** -0.5), to_chunks(v)
    ic = to_chunks(i_pre)
    lfc = to_chunks(jax.nn.log_sigmoid(f_pre))
    causal = jnp.tril(jnp.ones((CHUNK, CHUNK), dtype=bool))

    def step(carry, inp):
        c_st, n_st, m_st = carry
        qb, kb, vb, ib, lfb = inp
        b = jnp.cumsum(lfb, axis=-1)
        dlog = jnp.where(causal, b[..., :, None] - b[..., None, :] + ib[..., None, :], -jnp.inf)
        inter_log = b + m_st[..., None]
        m_t = jnp.maximum(jnp.max(dlog, axis=-1), inter_log)
        dw = jnp.exp(dlog - m_t[..., None])
        inter_w = jnp.exp(inter_log - m_t)
        s = jnp.einsum("bhid,bhjd->bhij", qb, kb) * dw
        num = jnp.einsum("bhij,bhje->bhie", s, vb) + inter_w[..., None] * jnp.einsum("bhid,bhde->bhie", qb, c_st)
        den = jnp.sum(s, axis=-1) + inter_w * jnp.einsum("bhid,bhd->bhi", qb, n_st)
        h = num / jnp.maximum(jnp.abs(den), jnp.exp(-m_t))[..., None]
        b_last = b[..., -1]
        w_log = b_last[..., None] - b + ib
        m_new = jnp.maximum(b_last + m_st, jnp.max(w_log, axis=-1))
        w = jnp.exp(w_log - m_new[..., None])
        dec = jnp.exp(b_last + m_st - m_new)
        c_st = dec[..., None, None] * c_st + jnp.einsum("bhj,bhjd,bhje->bhde", w, kb, vb)
        n_st = dec[..., None] * n_st + jnp.einsum("bhj,bhjd->bhd", w, kb)
        return (c_st, n_st, m_new), h

    init = (jnp.zeros((B, H, d, dv), F32), jnp.zeros((B, H, d), F32), jnp.zeros((B, H), F32))
    _, h = lax.scan(step, init, (qc, kc, vc, ic, lfc))
    return from_chunks(h)


def gated_delta_chunkwise(q, k, v, g, beta):
    B, S, H, d = q.shape
    dv = v.shape[-1]
    qc = to_chunks(l2norm(q) * d ** -0.5)
    kc = to_chunks(l2norm(k))
    vc = to_chunks(v)
    gc, bc = to_chunks(g), to_chunks(beta)
    G = jnp.cumsum(gc, axis=-1)
    lower = jnp.tril(jnp.ones((CHUNK, CHUNK), dtype=bool))
    strict = jnp.tril(jnp.ones((CHUNK, CHUNK), dtype=bool), k=-1)
    decay = jnp.exp(jnp.where(lower, G[..., :, None] - G[..., None, :], -jnp.inf))
    kb = kc * bc[..., None]
    a_mat = jnp.where(strict, jnp.einsum("nbhid,nbhjd->nbhij", kb, kc) * decay, 0.0)
    eye = jnp.eye(CHUNK, dtype=F32)
    rhs = jnp.concatenate([vc * bc[..., None], kb * jnp.exp(G)[..., None]], axis=-1)
    uw = lax.linalg.triangular_solve(a_mat + eye, rhs, left_side=True, lower=True, unit_diagonal=True)
    u, w = uw[..., :dv], uw[..., dv:]
    attn = jnp.einsum("nbhid,nbhjd->nbhij", qc, kc) * decay
    q_dec = qc * jnp.exp(G)[..., None]
    k_dec = kc * jnp.exp(G[..., -1:] - G)[..., None]
    g_last = jnp.exp(G[..., -1])

    def step(state, inp):
        ub, wb, ab, qd, kd, gl = inp
        v_new = ub - jnp.einsum("bhid,bhde->bhie", wb, state)
        o = jnp.einsum("bhid,bhde->bhie", qd, state) + jnp.einsum("bhij,bhje->bhie", ab, v_new)
        state = gl[..., None, None] * state + jnp.einsum("bhjd,bhje->bhde", kd, v_new)
        return state, o

    init = jnp.zeros((B, H, d, dv), F32)
    _, o = lax.scan(step, init, (u, w, attn, q_dec, k_dec, g_last))
    return from_chunks(o)


def retention_mlstm_mixer(hn, w_in, conv_w, b_i, b_f, gn_ret, gn_mlstm, w_out):
    z = hn @ w_in
    o0 = 0
    r_q = z[..., o0:o0 + RET_WIDTH]; o0 += RET_WIDTH
    r_k = z[..., o0:o0 + RET_WIDTH]; o0 += RET_WIDTH
    r_v = z[..., o0:o0 + RET_WIDTH]; o0 += RET_WIDTH
    r_g = z[..., o0:o0 + RET_WIDTH]; o0 += RET_WIDTH
    m_qk = z[..., o0:o0 + 2 * ML_WIDTH]; o0 += 2 * ML_WIDTH
    m_v = z[..., o0:o0 + ML_WIDTH]; o0 += ML_WIDTH
    m_o = z[..., o0:o0 + ML_WIDTH]; o0 += ML_WIDTH
    m_i = z[..., o0:o0 + ML_HEADS]; o0 += ML_HEADS
    m_f = z[..., o0:o0 + ML_HEADS]

    rq = rotary(split_heads(r_q.astype(F32), RET_HEADS))
    rk = rotary(split_heads(r_k.astype(F32), RET_HEADS))
    ret = retention_chunkwise(rq, rk, split_heads(r_v.astype(F32), RET_HEADS))
    ret = head_layernorm(ret, gn_ret) * jax.nn.silu(r_g.astype(F32))

    qk = jax.nn.silu(causal_conv(m_qk, conv_w).astype(F32))
    mq, mk = qk[..., :ML_WIDTH], qk[..., ML_WIDTH:]
    ml = mlstm_chunkwise(split_heads(mq, ML_HEADS), split_heads(mk, ML_HEADS),
                         split_heads(m_v.astype(F32), ML_HEADS),
                         m_i.astype(F32) + b_i.astype(F32), m_f.astype(F32) + b_f.astype(F32))
    ml = head_layernorm(ml, gn_mlstm) * jax.nn.sigmoid(m_o.astype(F32))

    mixed = jnp.concatenate([ret, ml], axis=-1).astype(hn.dtype)
    return mixed @ w_out


def gated_deltanet_mixer(hn, w_in, conv_w, a_log, dt_bias, norm_w, w_out):
    B, S, _ = hn.shape
    z = hn @ w_in
    qkv = jax.nn.silu(causal_conv(z[..., :3 * DN_WIDTH], conv_w).astype(F32))
    q = split_heads(qkv[..., :DN_WIDTH], DN_HEADS)
    k = split_heads(qkv[..., DN_WIDTH:2 * DN_WIDTH], DN_HEADS)
    v = split_heads(qkv[..., 2 * DN_WIDTH:], DN_HEADS)
    gate = split_heads(z[..., 3 * DN_WIDTH:4 * DN_WIDTH].astype(F32), DN_HEADS)
    beta = jax.nn.sigmoid(z[..., 4 * DN_WIDTH:4 * DN_WIDTH + DN_HEADS].astype(F32))
    a = z[..., 4 * DN_WIDTH + DN_HEADS:].astype(F32)
    g = -jnp.exp(a_log.astype(F32)) * jax.nn.softplus(a + dt_bias.astype(F32))
    o = gated_delta_chunkwise(q, k, v, g, beta)
    o = rmsnorm(o, norm_w) * jax.nn.silu(gate)
    return o.reshape(B, S, DN_WIDTH).astype(hn.dtype) @ w_out


def peer_ffn(hn, w_q, sub_keys, u_tab, v_tab):
    B, S, D = hn.shape
    T = B * S
    xt = hn.reshape(T, D)
    q = (xt @ w_q).astype(F32).reshape(T, PEER_HEADS, 2, PEER_HALF)
    scores = jnp.einsum("thpc,pnc->thpn", q, sub_keys.astype(F32))
    s_top, i_top = lax.top_k(scores, PEER_TOPK)
    cand = (s_top[:, :, 0, :, None] + s_top[:, :, 1, None, :]).reshape(T, PEER_HEADS, PEER_TOPK * PEER_TOPK)
    best, pos = lax.top_k(cand, PEER_TOPK)
    i1 = jnp.take_along_axis(i_top[:, :, 0, :], pos // PEER_TOPK, axis=-1)
    i2 = jnp.take_along_axis(i_top[:, :, 1, :], pos % PEER_TOPK, axis=-1)
    experts = i1 * PEER_NK + i2
    gates = jax.nn.softmax(best, axis=-1)
    nb = T // PEER_BLOCK

    def block(args):
        xb, eb, gb = args
        u = u_tab[eb]
        act = jnp.einsum("thkd,td->thk", u, xb).astype(F32)
        wgt = (gb * jax.nn.gelu(act, approximate=False)).astype(v_tab.dtype)
        return jnp.einsum("thk,thkd->td", wgt, v_tab[eb])

    out = lax.map(block, (xt.reshape(nb, PEER_BLOCK, D),
                          experts.reshape(nb, PEER_BLOCK, PEER_HEADS, PEER_TOPK),
                          gates.reshape(nb, PEER_BLOCK, PEER_HEADS, PEER_TOPK)))
    return out.reshape(B, S, D).astype(hn.dtype)


def per_layer_embedding(h, p_i, w_proj, w_gate, norm_w):
    gate = jax.nn.sigmoid((rmsnorm(h, norm_w) @ w_gate).astype(F32))
    e = (p_i.astype(h.dtype) @ w_proj).astype(F32)
    return (gate * e).astype(h.dtype)


def setup_inputs(seed: int = 0) -> dict:
    key = jax.random.key(seed)
    ks = jax.random.split(key, 32)
    nrm = jax.random.normal
    D = D_MODEL
    dt = jnp.exp(jax.random.uniform(ks[15], (N_ODD, DN_HEADS), minval=np.log(1e-3), maxval=np.log(1e-1)))
    return {
        "x": nrm(ks[0], (BATCH, SEQ, D), F32),
        "p": nrm(ks[1], (DEPTH, BATCH, SEQ, PLE_DIM), F32),
        "norm_mix_w": 1.0 + 0.01 * nrm(ks[2], (DEPTH, D), F32),
        "norm_ffn_w": 1.0 + 0.01 * nrm(ks[3], (DEPTH, D), F32),
        "ab_w_in": nrm(ks[4], (N_EVEN, D, AB_IN), F32) * D ** -0.5,
        "ab_conv_w": nrm(ks[5], (N_EVEN, CONV_K, 2 * ML_WIDTH), F32) * CONV_K ** -0.5,
        "ab_b_i": 0.1 * nrm(ks[6], (N_EVEN, ML_HEADS), F32),
        "ab_b_f": jnp.linspace(3.0, 6.0, ML_HEADS, dtype=F32)[None, :] + 0.1 * nrm(ks[7], (N_EVEN, ML_HEADS), F32),
        "ab_gn_ret": 1.0 + 0.01 * nrm(ks[8], (N_EVEN, RET_WIDTH), F32),
        "ab_gn_mlstm": 1.0 + 0.01 * nrm(ks[9], (N_EVEN, ML_WIDTH), F32),
        "ab_w_out": nrm(ks[10], (N_EVEN, AB_MIX, D), F32) * AB_MIX ** -0.5,
        "dn_w_in": nrm(ks[11], (N_ODD, D, DN_IN), F32) * D ** -0.5,
        "dn_conv_w": nrm(ks[12], (N_ODD, CONV_K, 3 * DN_WIDTH), F32) * CONV_K ** -0.5,
        "dn_a_log": jnp.log(jax.random.uniform(ks[13], (N_ODD, DN_HEADS), minval=1.0, maxval=16.0)),
        "dn_dt_bias": dt + jnp.log(-jnp.expm1(-dt)),
        "dn_norm_w": 1.0 + 0.01 * nrm(ks[14], (N_ODD, DN_DIM), F32),
        "dn_w_out": nrm(ks[16], (N_ODD, DN_WIDTH, D), F32) * DN_WIDTH ** -0.5,
        "peer_w_q": nrm(ks[17], (DEPTH, D, PEER_HEADS * PEER_QDIM), F32) * D ** -0.5,
        "peer_sub_keys": nrm(ks[18], (DEPTH, 2, PEER_NK, PEER_HALF), F32) * PEER_HALF ** -0.5,
        "peer_u": nrm(ks[19], (DEPTH, PEER_EXPERTS, D), F32) * D ** -0.5,
        "peer_v": nrm(ks[20], (DEPTH, PEER_EXPERTS, D), F32) * PEER_V_SCALE,
        "ple_w_proj": nrm(ks[21], (DEPTH, PLE_DIM, D), F32) * PLE_DIM ** -0.5,
        "ple_w_gate": nrm(ks[22], (DEPTH, D, D), F32) * D ** -0.5,
        "ple_norm_w": 1.0 + 0.01 * nrm(ks[23], (DEPTH, D), F32),
        "final_norm_w": 1.0 + 0.01 * nrm(ks[24], (D,), F32),
    }


def reference(x, p, norm_mix_w, norm_ffn_w, ab_w_in, ab_conv_w, ab_b_i, ab_b_f, ab_gn_ret,
              ab_gn_mlstm, ab_w_out, dn_w_in, dn_conv_w, dn_a_log, dn_dt_bias, dn_norm_w,
              dn_w_out, peer_w_q, peer_sub_keys, peer_u, peer_v, ple_w_proj, ple_w_gate,
              ple_norm_w, final_norm_w):
    h = x
    for i in range(DEPTH):
        hn = rmsnorm(h, norm_mix_w[i])
        j = i // 2
        if i % 2 == 0:
            h = h + retention_mlstm_mixer(hn, ab_w_in[j], ab_conv_w[j], ab_b_i[j], ab_b_f[j],
                                          ab_gn_ret[j], ab_gn_mlstm[j], ab_w_out[j])
        else:
            h = h + gated_deltanet_mixer(hn, dn_w_in[j], dn_conv_w[j], dn_a_log[j], dn_dt_bias[j],
                                         dn_norm_w[j], dn_w_out[j])
        h = h + peer_ffn(rmsnorm(h, norm_ffn_w[i]), peer_w_q[i], peer_sub_keys[i], peer_u[i], peer_v[i])
        h = h + per_layer_embedding(h, p[i], ple_w_proj[i], ple_w_gate[i], ple_norm_w[i])
    return rmsnorm(h, final_norm_w)
```

```python
import functools
import math

import jax
import jax.numpy as jnp
from jax import lax
from jax.experimental import pallas as pl
from jax.experimental.pallas import tpu as pltpu

F32 = jnp.float32
BF16 = jnp.bfloat16

LANES = 128
CHUNK = 128
HEAD_DIM = 128
CONV_K = 4
NORM_EPS = 1e-6
ROPE_BASE = 10000.0
RET_HEADS = 4
ML_HEADS = 4
DN_HEADS = 8
PEER_HEADS = 8
PEER_NK = 128
PEER_TOPK = 16
VMEM_LIMIT = 48 * 1024 * 1024

HIGHEST = lax.Precision.HIGHEST
NEG_INF = float("-inf")


def _dot(a, b, precision=None):
    return jnp.dot(a, b, preferred_element_type=F32, precision=precision)


def _dot_nt(a, b, precision=None):
    return lax.dot_general(a, b, (((1,), (1,)), ((), ())), preferred_element_type=F32,
                           precision=precision)


def _rms(x, w):
    return x * lax.rsqrt(jnp.mean(x * x, axis=-1, keepdims=True) + NORM_EPS) * w


def _sigmoid(x):
    return 1.0 / (1.0 + jnp.exp(-x))


def _silu(x):
    return x * _sigmoid(x)


def _softplus(x):
    return jnp.maximum(x, 0.0) + jnp.log(1.0 + jnp.exp(-jnp.abs(x)))


def _params(*sem):
    return pltpu.CompilerParams(dimension_semantics=sem, vmem_limit_bytes=VMEM_LIMIT)


def _pick_tile(n, cap):
    t = min(n, cap)
    while n % t:
        t //= 2
    return t


def _norm_matmul_kernel(x_ref, nw_ref, w_ref, o_ref, xn_ref):
    @pl.when(pl.program_id(1) == 0)
    def _():
        xn_ref[...] = _rms(x_ref[...], nw_ref[...]).astype(BF16)

    o_ref[...] = _dot(xn_ref[...], w_ref[...])


def _norm_matmul(h, norm_w, w):
    T, D = h.shape
    N = w.shape[1]
    tm = _pick_tile(T, 512)
    tn = N
    for cand in range(N // LANES, 0, -1):
        if (N // LANES) % cand == 0 and cand * LANES <= 1536:
            tn = cand * LANES
            break
    return pl.pallas_call(
        _norm_matmul_kernel,
        grid=(T // tm, N // tn),
        in_specs=[pl.BlockSpec((tm, D), lambda i, j: (i, 0)),
                  pl.BlockSpec((1, D), lambda i, j: (0, 0)),
                  pl.BlockSpec((D, tn), lambda i, j: (0, j))],
        out_specs=pl.BlockSpec((tm, tn), lambda i, j: (i, j)),
        out_shape=jax.ShapeDtypeStruct((T, N), F32),
        scratch_shapes=[pltpu.VMEM((tm, D), BF16)],
        compiler_params=_params("parallel", "arbitrary"),
    )(h, norm_w.reshape(1, D), w)


def _matmul_residual_kernel(a_ref, w_ref, h_ref, o_ref):
    o_ref[...] = h_ref[...] + _dot(a_ref[...], w_ref[...])


def _matmul_residual(a, w, h):
    T, K = a.shape
    D = w.shape[1]
    tm = _pick_tile(T, 512)
    return pl.pallas_call(
        _matmul_residual_kernel,
        grid=(T // tm,),
        in_specs=[pl.BlockSpec((tm, K), lambda i: (i, 0)),
                  pl.BlockSpec((K, D), lambda i: (0, 0)),
                  pl.BlockSpec((tm, D), lambda i: (i, 0))],
        out_specs=pl.BlockSpec((tm, D), lambda i: (i, 0)),
        out_shape=jax.ShapeDtypeStruct((T, D), F32),
        compiler_params=_params("parallel"),
    )(a, w, h)


def _causal_conv_silu(x, tail_ref, w_ref):
    tail = tail_ref[...]
    row = lax.broadcasted_iota(jnp.int32, (8, 1), 0)
    y = x * w_ref[CONV_K - 1:CONV_K, :]
    for j in range(1, CONV_K):
        xr = pltpu.roll(x, j, 0)
        top = jnp.where(row < j, pltpu.roll(tail, j, 0), xr[0:8])
        xs = jnp.concatenate([top, xr[8:]], axis=0)
        y = y + xs * w_ref[CONV_K - 1 - j:CONV_K - j, :]
    tail_ref[...] = x[CHUNK - 8:CHUNK]
    return _silu(y)


def _tril(dtype=F32, k=0):
    r = lax.broadcasted_iota(jnp.int32, (CHUNK, CHUNK), 0)
    c = lax.broadcasted_iota(jnp.int32, (CHUNK, CHUNK), 1)
    return (r - c >= k) if dtype is None else (r - c >= k).astype(dtype)


def _head_layernorm(y):
    mu = jnp.mean(y, axis=-1, keepdims=True)
    yc = y - mu
    return yc * lax.rsqrt(jnp.mean(yc * yc, axis=-1, keepdims=True) + NORM_EPS)


def _ab_mixer_kernel(rq_ref, rk_ref, rv_ref, rg_ref, mq_ref, mk_ref, mv_ref, mo_ref, g_ref,
                     cos_ref, sin_ref, dmat_ref, xi_ref, zeta_ref, gch_ref,
                     convq_ref, convk_ref, gbias_ref, gnr_ref, gnm_ref,
                     o_ref, rstate_ref, mstate_ref, mmax_ref, tailq_ref, tailk_ref):
    c = pl.program_id(1)

    @pl.when(c == 0)
    def _():
        rstate_ref[...] = jnp.zeros_like(rstate_ref)
        mstate_ref[...] = jnp.zeros_like(mstate_ref)
        mmax_ref[...] = jnp.zeros_like(mmax_ref)
        tailq_ref[...] = jnp.zeros_like(tailq_ref)
        tailk_ref[...] = jnp.zeros_like(tailk_ref)

    scale = HEAD_DIM ** -0.5
    cos = cos_ref[...]
    sin = sin_ref[...]

    for h in range(RET_HEADS):
        sl = slice(h * HEAD_DIM, (h + 1) * HEAD_DIM)
        q = rq_ref[:, sl]
        k = rk_ref[:, sl]
        v = rv_ref[:, sl]
        q = q * cos + pltpu.roll(q, HEAD_DIM // 2, 1) * sin
        k = (k * cos + pltpu.roll(k, HEAD_DIM // 2, 1) * sin) * scale
        qb, kb, vb = q.astype(BF16), k.astype(BF16), v.astype(BF16)
        st = rstate_ref[h]
        s = _dot_nt(qb, kb) * dmat_ref[h]
        o = _dot(s.astype(BF16), vb) + _dot(qb, st.astype(BF16)) * xi_ref[:, h:h + 1]
        kz = (k * zeta_ref[:, h:h + 1]).T
        rstate_ref[h] = gch_ref[h] * st + _dot(kz.astype(BF16), vb)
        y = _head_layernorm(o) * gnr_ref[:, sl] * _silu(rg_ref[:, sl])
        o_ref[:, sl] = y.astype(o_ref.dtype)

    mq = _causal_conv_silu(mq_ref[...], tailq_ref, convq_ref)
    mk = _causal_conv_silu(mk_ref[...], tailk_ref, convk_ref) * scale
    gates = g_ref[...] + gbias_ref[...]
    lf = jnp.minimum(gates, 0.0) - jnp.log(1.0 + jnp.exp(-jnp.abs(gates)))
    bcum = _dot(_tril(), lf, precision=HIGHEST)
    gates_t = gates.T
    bcum_t = bcum.T
    causal = _tril(None)
    lane = lax.broadcasted_iota(jnp.int32, (CHUNK, 2 * HEAD_DIM), 1)
    for h in range(ML_HEADS):
        sl = slice(h * HEAD_DIM, (h + 1) * HEAD_DIM)
        q = mq[:, sl]
        k = mk[:, sl]
        v = mv_ref[:, sl]
        i_col = gates[:, h:h + 1]
        i_row = gates_t[h:h + 1, :]
        b_col = bcum[:, ML_HEADS + h:ML_HEADS + h + 1]
        b_row = bcum_t[ML_HEADS + h:ML_HEADS + h + 1, :]
        m_st = mmax_ref[h]
        dlog = jnp.where(causal, b_col - b_row + i_row, NEG_INF)
        inter_log = b_col + m_st
        m_t = jnp.maximum(jnp.max(dlog, axis=-1, keepdims=True), inter_log)
        dw = jnp.exp(dlog - m_t)
        inter_w = jnp.exp(inter_log - m_t)
        qb, kb = q.astype(BF16), k.astype(BF16)
        v_aug = jnp.where(lane < HEAD_DIM, jnp.concatenate([v, v], axis=1),
                          (lane == HEAD_DIM).astype(F32)).astype(BF16)
        st = mstate_ref[h]
        s = _dot_nt(qb, kb) * dw
        nd = _dot(s.astype(BF16), v_aug) + inter_w * _dot(qb, st.astype(BF16))
        num = nd[:, :HEAD_DIM]
        den = nd[:, HEAD_DIM:HEAD_DIM + 1]
        hh = num / jnp.maximum(jnp.abs(den), jnp.exp(-m_t))
        b_last = b_col[CHUNK - 1:CHUNK, :]
        w_log = b_last - b_col + i_col
        m_new = jnp.maximum(b_last + m_st, jnp.max(w_log, axis=0, keepdims=True))
        w = jnp.exp(w_log - m_new)
        dec = jnp.exp(b_last + m_st - m_new)
        kw = (k * w).T
        mstate_ref[h] = dec * st + _dot(kw.astype(BF16), v_aug)
        mmax_ref[h] = m_new
        osl = slice(RET_HEADS * HEAD_DIM + h * HEAD_DIM, RET_HEADS * HEAD_DIM + (h + 1) * HEAD_DIM)
        y = _head_layernorm(hh) * gnm_ref[:, sl] * _sigmoid(mo_ref[:, sl])
        o_ref[:, osl] = y.astype(o_ref.dtype)


def _ab_mixer(z, B, S, conv_w, b_i, b_f, gn_ret, gn_mlstm):
    N = z.shape[1]
    z3 = z.reshape(B, S, N)
    nc = S // CHUNK
    rw = RET_HEADS * HEAD_DIM
    mw = ML_HEADS * HEAD_DIM

    pos = jnp.arange(S, dtype=F32)
    inv = 1.0 / (ROPE_BASE ** (jnp.arange(0, HEAD_DIM, 2, dtype=F32) / HEAD_DIM))
    ang = pos[:, None] * inv[None, :]
    cos = jnp.concatenate([jnp.cos(ang), jnp.cos(ang)], axis=1)
    sin = jnp.concatenate([-jnp.sin(ang), jnp.sin(ang)], axis=1)
    log_g = jnp.log(1.0 - 2.0 ** (-5.0 - jnp.arange(RET_HEADS, dtype=F32)))
    idx = jnp.arange(CHUNK, dtype=F32)
    diff = idx[:, None] - idx[None, :]
    cm = diff >= 0
    dmat = jnp.where(cm[None], jnp.exp(jnp.where(cm, diff, 0.0)[None] * log_g[:, None, None]), 0.0)
    xi = jnp.exp((idx + 1.0)[:, None] * log_g[None, :])
    zeta = jnp.exp((CHUNK - 1.0 - idx)[:, None] * log_g[None, :])
    gch = jnp.exp(CHUNK * log_g).reshape(RET_HEADS, 1, 1)
    gbias = jnp.zeros((1, LANES), F32).at[0, :ML_HEADS].set(b_i).at[0, ML_HEADS:2 * ML_HEADS].set(b_f)

    def zcol(width, blk):
        return pl.BlockSpec((None, CHUNK, width), lambda b, c: (b, c, blk))

    def full(shape):
        return pl.BlockSpec(shape, lambda b, c: (0,) * len(shape))

    return pl.pallas_call(
        _ab_mixer_kernel,
        grid=(B, nc),
        in_specs=[zcol(rw, 0), zcol(rw, 1), zcol(rw, 2), zcol(rw, 3),
                  zcol(mw, 4), zcol(mw, 5), zcol(mw, 6), zcol(mw, 7),
                  zcol(LANES, (4 * rw + 4 * mw) // LANES),
                  pl.BlockSpec((CHUNK, HEAD_DIM), lambda b, c: (c, 0)),
                  pl.BlockSpec((CHUNK, HEAD_DIM), lambda b, c: (c, 0)),
                  full((RET_HEADS, CHUNK, CHUNK)), full((CHUNK, RET_HEADS)), full((CHUNK, RET_HEADS)),
                  full((RET_HEADS, 1, 1)),
                  full((CONV_K, mw)), full((CONV_K, mw)), full((1, LANES)),
                  full((1, rw)), full((1, mw))],
        out_specs=pl.BlockSpec((None, CHUNK, rw + mw), lambda b, c: (b, c, 0)),
        out_shape=jax.ShapeDtypeStruct((B, S, rw + mw), BF16),
        scratch_shapes=[pltpu.VMEM((RET_HEADS, HEAD_DIM, HEAD_DIM), F32),
                        pltpu.VMEM((ML_HEADS, HEAD_DIM, 2 * HEAD_DIM), F32),
                        pltpu.VMEM((ML_HEADS, 1, 1), F32),
                        pltpu.VMEM((8, mw), F32),
                        pltpu.VMEM((8, mw), F32)],
        compiler_params=_params("parallel", "arbitrary"),
    )(z3, z3, z3, z3, z3, z3, z3, z3, z3, cos, sin, dmat, xi, zeta, gch,
      conv_w[:, :mw], conv_w[:, mw:], gbias, gn_ret.reshape(1, rw), gn_mlstm.reshape(1, mw)
      ).reshape(B * S, rw + mw)


def _unit_lower_inverse(a):
    r = lax.broadcasted_iota(jnp.int32, (CHUNK, CHUNK), 0)
    c = lax.broadcasted_iota(jnp.int32, (CHUNK, CHUNK), 1)
    p = -a
    t = (r == c).astype(F32) + p
    n = 2
    while n < CHUNK:
        p = _dot(p, p, precision=HIGHEST)
        t = t + _dot(t, p, precision=HIGHEST)
        n *= 2
    return t


def _dn_mixer_kernel(qkv_ref, gate_ref, g_ref, conv_ref, gpar_ref, nw_ref,
                     o_ref, state_ref, tail_ref):
    c = pl.program_id(1)

    @pl.when(c == 0)
    def _():
        state_ref[...] = jnp.zeros_like(state_ref)
        tail_ref[...] = jnp.zeros_like(tail_ref)

    w = DN_HEADS * HEAD_DIM
    qkv = _causal_conv_silu(qkv_ref[...], tail_ref, conv_ref)
    gates = g_ref[...]
    beta_all = _sigmoid(gates)
    g_all = -jnp.exp(gpar_ref[1:2, :]) * _softplus(gates + gpar_ref[0:1, :])
    gcum = _dot(_tril(), g_all, precision=HIGHEST)
    gcum_t = gcum.T
    lower = _tril(None)
    strict = _tril(None, 1)
    for h in range(DN_HEADS):
        sl = slice(h * HEAD_DIM, (h + 1) * HEAD_DIM)
        q = qkv[:, sl]
        k = qkv[:, w + h * HEAD_DIM:w + (h + 1) * HEAD_DIM]
        v = qkv[:, 2 * w + h * HEAD_DIM:2 * w + (h + 1) * HEAD_DIM]
        q = q * lax.rsqrt(jnp.sum(q * q, axis=-1, keepdims=True) + NORM_EPS) * HEAD_DIM ** -0.5
        k = k * lax.rsqrt(jnp.sum(k * k, axis=-1, keepdims=True) + NORM_EPS)
        beta = beta_all[:, h:h + 1]
        g_col = gcum[:, DN_HEADS + h:DN_HEADS + h + 1]
        g_row = gcum_t[DN_HEADS + h:DN_HEADS + h + 1, :]
        g_last = g_col[CHUNK - 1:CHUNK, :]
        decay = jnp.exp(jnp.where(lower, g_col - g_row, NEG_INF))
        kb = k * beta
        kbf = k.astype(BF16)
        a_mat = jnp.where(strict, _dot_nt(kb.astype(BF16), kbf) * decay, 0.0)
        t_inv = _unit_lower_inverse(a_mat)
        eg = jnp.exp(g_col)
        rhs = jnp.concatenate([v * beta, kb * eg], axis=1)
        uw = _dot(t_inv, rhs, precision=HIGHEST)
        u = uw[:, :HEAD_DIM]
        wm = uw[:, HEAD_DIM:]
        attn = _dot_nt(q.astype(BF16), kbf) * decay
        q_dec = q * eg
        k_dec = k * jnp.exp(g_last - g_col)
        st = state_ref[h]
        stb = st.astype(BF16)
        v_new = u - _dot(wm.astype(BF16), stb)
        vnb = v_new.astype(BF16)
        o = _dot(q_dec.astype(BF16), stb) + _dot(attn.astype(BF16), vnb)
        state_ref[h] = jnp.exp(g_last) * st + _dot(k_dec.T.astype(BF16), vnb)
        y = _rms(o, nw_ref[...]) * _silu(gate_ref[:, sl])
        o_ref[:, sl] = y.astype(o_ref.dtype)


def _dn_mixer(z, B, S, conv_w, a_log, dt_bias, norm_w):
    N = z.shape[1]
    z3 = z.reshape(B, S, N)
    nc = S // CHUNK
    w = DN_HEADS * HEAD_DIM
    gpar = jnp.zeros((8, LANES), F32)
    gpar = gpar.at[0, DN_HEADS:2 * DN_HEADS].set(dt_bias).at[1, DN_HEADS:2 * DN_HEADS].set(a_log)

    def full(shape):
        return pl.BlockSpec(shape, lambda b, c: (0,) * len(shape))

    return pl.pallas_call(
        _dn_mixer_kernel,
        grid=(B, nc),
        in_specs=[pl.BlockSpec((None, CHUNK, 3 * w), lambda b, c: (b, c, 0)),
                  pl.BlockSpec((None, CHUNK, w), lambda b, c: (b, c, 3)),
                  pl.BlockSpec((None, CHUNK, LANES), lambda b, c: (b, c, 4 * w // LANES)),
                  full((CONV_K, 3 * w)), full((8, LANES)), full((1, HEAD_DIM))],
        out_specs=pl.BlockSpec((None, CHUNK, w), lambda b, c: (b, c, 0)),
        out_shape=jax.ShapeDtypeStruct((B, S, w), BF16),
        scratch_shapes=[pltpu.VMEM((DN_HEADS, HEAD_DIM, HEAD_DIM), F32),
                        pltpu.VMEM((8, 3 * w), F32)],
        compiler_params=_params("parallel", "arbitrary"),
    )(z3, z3, z3, conv_w, gpar, norm_w.reshape(1, HEAD_DIM)).reshape(B * S, w)


def _candidate_pairs():
    return [(i, j) for i in range(PEER_TOPK) for j in range(PEER_TOPK)
            if (i + 1) * (j + 1) <= PEER_TOPK]


def _peer_route_kernel(h_ref, nw_ref, wq_ref, keys_ref, xn_ref, s1_ref, s2_ref, st_ref, top_ref):
    xn = _rms(h_ref[...], nw_ref[...])
    xn_t = xn.T.astype(BF16)
    xn_ref[...] = xn_t
    q_t = _dot(wq_ref[...], xn_t)
    for h in range(PEER_HEADS):
        for p in range(2):
            lo = (2 * h + p) * PEER_NK
            s = _dot(keys_ref[p], q_t[lo:lo + PEER_NK].astype(BF16))
            (s1_ref if p == 0 else s2_ref)[h] = s

            def extract(r, cur, h=h, p=p):
                m = jnp.max(cur, axis=0, keepdims=True)
                top_ref[p, pl.ds(r, 1), h:h + 1, :] = m[None]
                return jnp.where(cur >= m, NEG_INF, cur)

            lax.fori_loop(0, PEER_TOPK, extract, s)

    a = [top_ref[0, r] for r in range(PEER_TOPK)]
    b = [top_ref[1, r] for r in range(PEER_TOPK)]
    cands = [a[i] + b[j] for i, j in _candidate_pairs()]
    cur = list(cands)
    for _ in range(PEER_TOPK - 1):
        m = functools.reduce(jnp.maximum, cur)
        cur = [jnp.where(x >= m, NEG_INF, x) for x in cur]
    theta = functools.reduce(jnp.maximum, cur)
    top = a[0] + b[0]
    z = functools.reduce(lambda u, v: u + v,
                         [jnp.where(x >= theta, jnp.exp(x - top), 0.0) for x in cands])
    st_ref[0:8, :] = theta
    st_ref[8:16, :] = a[0]
    st_ref[16:24, :] = b[0]
    st_ref[24:32, :] = 1.0 / z


def _peer_route(h, norm_w, wq_t, keys):
    T, D = h.shape
    tt = _pick_tile(T, 256)
    nq = wq_t.shape[0]
    return pl.pallas_call(
        _peer_route_kernel,
        grid=(T // tt,),
        in_specs=[pl.BlockSpec((tt, D), lambda i: (i, 0)),
                  pl.BlockSpec((1, D), lambda i: (0, 0)),
                  pl.BlockSpec((nq, D), lambda i: (0, 0)),
                  pl.BlockSpec((2, PEER_NK, PEER_NK), lambda i: (0, 0, 0))],
        out_specs=[pl.BlockSpec((D, tt), lambda i: (0, i)),
                   pl.BlockSpec((PEER_HEADS, PEER_NK, tt), lambda i: (0, 0, i)),
                   pl.BlockSpec((PEER_HEADS, PEER_NK, tt), lambda i: (0, 0, i)),
                   pl.BlockSpec((4 * PEER_HEADS, tt), lambda i: (0, i))],
        out_shape=[jax.ShapeDtypeStruct((D, T), BF16),
                   jax.ShapeDtypeStruct((PEER_HEADS, PEER_NK, T), F32),
                   jax.ShapeDtypeStruct((PEER_HEADS, PEER_NK, T), F32),
                   jax.ShapeDtypeStruct((4 * PEER_HEADS, T), F32)],
        scratch_shapes=[pltpu.VMEM((2, PEER_TOPK, PEER_HEADS, tt), F32)],
        compiler_params=_params("parallel"),
    )(h, norm_w.reshape(1, D), wq_t, keys)


def _peer_dense_kernel(xn_ref, u_ref, vt_ref, s1_ref, s2_ref, st_ref, h_ref, o_ref,
                       acc_ref, e1_ref, e2_ref, wg_ref, *, sub):
    j = pl.program_id(1)
    eb, tt = wg_ref.shape
    nblk = eb // PEER_NK

    @pl.when(j == 0)
    def _():
        acc_ref[...] = jnp.zeros_like(acc_ref)
        for h in range(PEER_HEADS):
            e1_ref[h] = jnp.exp(s1_ref[h] - st_ref[8 + h:9 + h, :]) * st_ref[24 + h:25 + h, :]
            e2_ref[h] = jnp.exp(s2_ref[h] - st_ref[16 + h:17 + h, :])

    act = _dot(u_ref[...], xn_ref[...])
    for kb in range(nblk):
        i1 = j * nblk + kb
        for r0 in range(0, PEER_NK, sub):
            w = jnp.zeros((sub, tt), F32)
            for h in range(PEER_HEADS):
                s1row = s1_ref[h, pl.ds(i1, 1), :]
                e1row = e1_ref[h, pl.ds(i1, 1), :]
                sel = (s2_ref[h, r0:r0 + sub, :] + s1row) >= st_ref[h:h + 1, :]
                w = w + jnp.where(sel, e2_ref[h, r0:r0 + sub, :], 0.0) * e1row
            a = act[kb * PEER_NK + r0:kb * PEER_NK + r0 + sub, :]
            g = 0.5 * a * (1.0 + lax.erf(a * (1.0 / math.sqrt(2.0))))
            wg_ref[kb * PEER_NK + r0:kb * PEER_NK + r0 + sub, :] = (w * g).astype(BF16)
    acc_ref[...] += _dot(vt_ref[...], wg_ref[...])

    @pl.when(j == pl.num_programs(1) - 1)
    def _():
        o_ref[...] = h_ref[...] + acc_ref[...].T


def _peer_dense(xn_t, u, v_t, s1, s2, stats, h, *, tt_cap=512, eb=512, sub=32):
    D, T = xn_t.shape
    E = u.shape[0]
    tt = _pick_tile(T, tt_cap)
    hs = (PEER_HEADS, PEER_NK, tt)
    return pl.pallas_call(
        functools.partial(_peer_dense_kernel, sub=sub),
        grid=(T // tt, E // eb),
        in_specs=[pl.BlockSpec((D, tt), lambda i, j: (0, i)),
                  pl.BlockSpec((eb, D), lambda i, j: (j, 0)),
                  pl.BlockSpec((D, eb), lambda i, j: (0, j)),
                  pl.BlockSpec(hs, lambda i, j: (0, 0, i)),
                  pl.BlockSpec(hs, lambda i, j: (0, 0, i)),
                  pl.BlockSpec((4 * PEER_HEADS, tt), lambda i, j: (0, i)),
                  pl.BlockSpec((tt, D), lambda i, j: (i, 0))],
        out_specs=pl.BlockSpec((tt, D), lambda i, j: (i, 0)),
        out_shape=jax.ShapeDtypeStruct((T, D), F32),
        scratch_shapes=[pltpu.VMEM((D, tt), F32),
                        pltpu.VMEM(hs, F32), pltpu.VMEM(hs, F32),
                        pltpu.VMEM((eb, tt), BF16)],
        compiler_params=_params("parallel", "arbitrary"),
    )(xn_t, u, v_t, s1, s2, stats, h)


def _ple_kernel(h_ref, p_ref, nw_ref, wg_ref, wp_ref, fw_ref, o_ref, *, final_norm):
    h = h_ref[...]
    gate = _sigmoid(_dot(_rms(h, nw_ref[...]).astype(BF16), wg_ref[...]))
    e = _dot(p_ref[...].astype(BF16), wp_ref[...])
    out = h + gate * e
    if final_norm:
        out = _rms(out, fw_ref[...])
    o_ref[...] = out


def _ple(h, p, norm_w, w_gate, w_proj, final_w, final_norm):
    T, D = h.shape
    P = p.shape[1]
    tm = _pick_tile(T, 512)
    return pl.pallas_call(
        functools.partial(_ple_kernel, final_norm=final_norm),
        grid=(T // tm,),
        in_specs=[pl.BlockSpec((tm, D), lambda i: (i, 0)),
                  pl.BlockSpec((tm, P), lambda i: (i, 0)),
                  pl.BlockSpec((1, D), lambda i: (0, 0)),
                  pl.BlockSpec((D, D), lambda i: (0, 0)),
                  pl.BlockSpec((P, D), lambda i: (0, 0)),
                  pl.BlockSpec((1, D), lambda i: (0, 0))],
        out_specs=pl.BlockSpec((tm, D), lambda i: (i, 0)),
        out_shape=jax.ShapeDtypeStruct((T, D), F32),
        compiler_params=_params("parallel"),
    )(h, p, norm_w.reshape(1, D), w_gate, w_proj, final_w.reshape(1, D))


def _pad_cols(w, mult=LANES):
    n = w.shape[1]
    return jnp.pad(w, ((0, 0), (0, (-n) % mult)))


def kernel(x, p, norm_mix_w, norm_ffn_w, ab_w_in, ab_conv_w, ab_b_i, ab_b_f, ab_gn_ret, ab_gn_mlstm, ab_w_out, dn_w_in, dn_conv_w, dn_a_log, dn_dt_bias, dn_norm_w, dn_w_out, peer_w_q, peer_sub_keys, peer_u, peer_v, ple_w_proj, ple_w_gate, ple_norm_w, final_norm_w):
    B, S, D = x.shape
    depth = p.shape[0]
    T = B * S
    h = x.reshape(T, D)
    for i in range(depth):
        j = i // 2
        if i % 2 == 0:
            z = _norm_matmul(h, norm_mix_w[i], _pad_cols(ab_w_in[j]).astype(BF16))
            mixed = _ab_mixer(z, B, S, ab_conv_w[j], ab_b_i[j], ab_b_f[j], ab_gn_ret[j], ab_gn_mlstm[j])
            h = _matmul_residual(mixed, ab_w_out[j].astype(BF16), h)
        else:
            z = _norm_matmul(h, norm_mix_w[i], _pad_cols(dn_w_in[j]).astype(BF16))
            mixed = _dn_mixer(z, B, S, dn_conv_w[j], dn_a_log[j], dn_dt_bias[j], dn_norm_w[j])
            h = _matmul_residual(mixed, dn_w_out[j].astype(BF16), h)
        xn_t, s1, s2, stats = _peer_route(h, norm_ffn_w[i], peer_w_q[i].T.astype(BF16),
                                          peer_sub_keys[i].astype(BF16))
        h = _peer_dense(xn_t, peer_u[i].astype(BF16), peer_v[i].T.astype(BF16), s1, s2, stats, h)
        h = _ple(h, p[i].reshape(T, -1), ple_norm_w[i], ple_w_gate[i].astype(BF16),
                 ple_w_proj[i].astype(BF16), final_norm_w, final_norm=(i == depth - 1))
    return h.reshape(B, S, D)
```

```python
import functools
import math

import jax
import jax.numpy as jnp
from jax import lax
from jax.experimental import pallas as pl
from jax.experimental.pallas import tpu as pltpu

F32 = jnp.float32
BF16 = jnp.bfloat16

LANES = 128
SUBLANES = 8
MXU_N = 256
CHUNK = 128
HEAD_DIM = 128
CONV_K = 4
NORM_EPS = 1e-6
ROPE_BASE = 10000.0
RET_HEADS = 4
ML_HEADS = 4
DN_HEADS = 8
PEER_HEADS = 8
PEER_NK = 128
PEER_TOPK = 16
VMEM_LIMIT = 48 * 1024 * 1024

HIGHEST = lax.Precision.HIGHEST
NEG_INF = float("-inf")


def _dot(a, b, precision=None):
    return jnp.dot(a, b, preferred_element_type=F32, precision=precision)


def _dot_nt(a, b, precision=None):
    return lax.dot_general(a, b, (((1,), (1,)), ((), ())), preferred_element_type=F32,
                           precision=precision)


def _rms(x, w):
    return x * lax.rsqrt(jnp.mean(x * x, axis=-1, keepdims=True) + NORM_EPS) * w


def _sigmoid(x):
    return 1.0 / (1.0 + jnp.exp(-x))


def _silu(x):
    return x * _sigmoid(x)


def _softplus(x):
    return jnp.maximum(x, 0.0) + jnp.log(1.0 + jnp.exp(-jnp.abs(x)))


def _params(*sem, flags=None):
    return pltpu.CompilerParams(dimension_semantics=sem, vmem_limit_bytes=VMEM_LIMIT, flags=flags)


def _pick_tile(n, cap):
    t = min(n, cap)
    while n % t:
        t //= 2
    return t


def _norm_matmul_kernel(x_ref, nw_ref, w_ref, o_ref, xn_ref):
    @pl.when(pl.program_id(1) == 0)
    def _():
        xn_ref[...] = _rms(x_ref[...], nw_ref[...]).astype(BF16)

    o_ref[...] = _dot(xn_ref[...], w_ref[...])


def _norm_matmul(h, norm_w, w):
    T, D = h.shape
    N = w.shape[1]
    tm = _pick_tile(T, 512)
    tn = N
    for cand in range(N // LANES, 0, -1):
        if (N // LANES) % cand == 0 and cand * LANES <= 1536:
            tn = cand * LANES
            break
    return pl.pallas_call(
        _norm_matmul_kernel,
        grid=(T // tm, N // tn),
        in_specs=[pl.BlockSpec((tm, D), lambda i, j: (i, 0)),
                  pl.BlockSpec((1, D), lambda i, j: (0, 0)),
                  pl.BlockSpec((D, tn), lambda i, j: (0, j))],
        out_specs=pl.BlockSpec((tm, tn), lambda i, j: (i, j)),
        out_shape=jax.ShapeDtypeStruct((T, N), F32),
        scratch_shapes=[pltpu.VMEM((tm, D), BF16)],
        compiler_params=_params("parallel", "arbitrary"),
        name="norm_matmul",
    )(h, norm_w.reshape(1, D), w)


def _matmul_residual_kernel(a_ref, w_ref, h_ref, o_ref):
    o_ref[...] = h_ref[...] + _dot(a_ref[...], w_ref[...])


def _matmul_residual(a, w, h):
    T, K = a.shape
    D = w.shape[1]
    tm = _pick_tile(T, 512)
    return pl.pallas_call(
        _matmul_residual_kernel,
        grid=(T // tm,),
        in_specs=[pl.BlockSpec((tm, K), lambda i: (i, 0)),
                  pl.BlockSpec((K, D), lambda i: (0, 0)),
                  pl.BlockSpec((tm, D), lambda i: (i, 0))],
        out_specs=pl.BlockSpec((tm, D), lambda i: (i, 0)),
        out_shape=jax.ShapeDtypeStruct((T, D), F32),
        compiler_params=_params("parallel"),
        name="matmul_residual",
    )(a, w, h)


def _causal_conv_silu(x, tail_ref, w_ref):
    tail = tail_ref[...]
    row = lax.broadcasted_iota(jnp.int32, (8, 1), 0)
    y = x * w_ref[CONV_K - 1:CONV_K, :]
    for j in range(1, CONV_K):
        xr = pltpu.roll(x, j, 0)
        top = jnp.where(row < j, pltpu.roll(tail, j, 0), xr[0:8])
        xs = jnp.concatenate([top, xr[8:]], axis=0)
        y = y + xs * w_ref[CONV_K - 1 - j:CONV_K - j, :]
    tail_ref[...] = x[CHUNK - 8:CHUNK]
    return _silu(y)


def _tril(dtype=F32, k=0):
    r = lax.broadcasted_iota(jnp.int32, (CHUNK, CHUNK), 0)
    c = lax.broadcasted_iota(jnp.int32, (CHUNK, CHUNK), 1)
    return (r - c >= k) if dtype is None else (r - c >= k).astype(dtype)


def _head_layernorm(y):
    mu = jnp.mean(y, axis=-1, keepdims=True)
    yc = y - mu
    return yc * lax.rsqrt(jnp.mean(yc * yc, axis=-1, keepdims=True) + NORM_EPS)


def _ab_mixer_kernel(rq_ref, rk_ref, rv_ref, rg_ref, mq_ref, mk_ref, mv_ref, mo_ref, g_ref,
                     cos_ref, sin_ref, dmat_ref, xi_ref, zeta_ref, gch_ref,
                     convq_ref, convk_ref, gbias_ref, gnr_ref, gnm_ref,
                     o_ref, rstate_ref, mstate_ref, mmax_ref, tailq_ref, tailk_ref):
    c = pl.program_id(1)

    @pl.when(c == 0)
    def _():
        rstate_ref[...] = jnp.zeros_like(rstate_ref)
        mstate_ref[...] = jnp.zeros_like(mstate_ref)
        mmax_ref[...] = jnp.zeros_like(mmax_ref)
        tailq_ref[...] = jnp.zeros_like(tailq_ref)
        tailk_ref[...] = jnp.zeros_like(tailk_ref)

    scale = HEAD_DIM ** -0.5
    cos = cos_ref[...]
    sin = sin_ref[...]

    for h in range(RET_HEADS):
        sl = slice(h * HEAD_DIM, (h + 1) * HEAD_DIM)
        q = rq_ref[:, sl]
        k = rk_ref[:, sl]
        v = rv_ref[:, sl]
        q = q * cos + pltpu.roll(q, HEAD_DIM // 2, 1) * sin
        k = (k * cos + pltpu.roll(k, HEAD_DIM // 2, 1) * sin) * scale
        qb, kb, vb = q.astype(BF16), k.astype(BF16), v.astype(BF16)
        st = rstate_ref[h]
        s = _dot_nt(qb, kb) * dmat_ref[h]
        o = _dot(s.astype(BF16), vb) + _dot(qb, st.astype(BF16)) * xi_ref[:, h:h + 1]
        kz = (k * zeta_ref[:, h:h + 1]).T
        rstate_ref[h] = gch_ref[h] * st + _dot(kz.astype(BF16), vb)
        y = _head_layernorm(o) * gnr_ref[:, sl] * _silu(rg_ref[:, sl])
        o_ref[:, sl] = y.astype(o_ref.dtype)

    mq = _causal_conv_silu(mq_ref[...], tailq_ref, convq_ref)
    mk = _causal_conv_silu(mk_ref[...], tailk_ref, convk_ref) * scale
    gates = g_ref[...] + gbias_ref[...]
    lf = jnp.minimum(gates, 0.0) - jnp.log(1.0 + jnp.exp(-jnp.abs(gates)))
    bcum = _dot(_tril(), lf, precision=HIGHEST)
    gates_t = gates.T
    bcum_t = bcum.T
    causal = _tril(None)
    lane = lax.broadcasted_iota(jnp.int32, (CHUNK, 2 * HEAD_DIM), 1)
    for h in range(ML_HEADS):
        sl = slice(h * HEAD_DIM, (h + 1) * HEAD_DIM)
        q = mq[:, sl]
        k = mk[:, sl]
        v = mv_ref[:, sl]
        i_col = gates[:, h:h + 1]
        i_row = gates_t[h:h + 1, :]
        b_col = bcum[:, ML_HEADS + h:ML_HEADS + h + 1]
        b_row = bcum_t[ML_HEADS + h:ML_HEADS + h + 1, :]
        m_st = mmax_ref[h]
        dlog = jnp.where(causal, b_col - b_row + i_row, NEG_INF)
        inter_log = b_col + m_st
        m_t = jnp.maximum(jnp.max(dlog, axis=-1, keepdims=True), inter_log)
        dw = jnp.exp(dlog - m_t)
        inter_w = jnp.exp(inter_log - m_t)
        qb, kb = q.astype(BF16), k.astype(BF16)
        v_aug = jnp.where(lane < HEAD_DIM, jnp.concatenate([v, v], axis=1),
                          (lane == HEAD_DIM).astype(F32)).astype(BF16)
        st = mstate_ref[h]
        s = _dot_nt(qb, kb) * dw
        nd = _dot(s.astype(BF16), v_aug) + inter_w * _dot(qb, st.astype(BF16))
        num = nd[:, :HEAD_DIM]
        den = nd[:, HEAD_DIM:HEAD_DIM + 1]
        hh = num / jnp.maximum(jnp.abs(den), jnp.exp(-m_t))
        b_last = b_col[CHUNK - 1:CHUNK, :]
        w_log = b_last - b_col + i_col
        m_new = jnp.maximum(b_last + m_st, jnp.max(w_log, axis=0, keepdims=True))
        w = jnp.exp(w_log - m_new)
        dec = jnp.exp(b_last + m_st - m_new)
        kw = (k * w).T
        mstate_ref[h] = dec * st + _dot(kw.astype(BF16), v_aug)
        mmax_ref[h] = m_new
        osl = slice(RET_HEADS * HEAD_DIM + h * HEAD_DIM, RET_HEADS * HEAD_DIM + (h + 1) * HEAD_DIM)
        y = _head_layernorm(hh) * gnm_ref[:, sl] * _sigmoid(mo_ref[:, sl])
        o_ref[:, osl] = y.astype(o_ref.dtype)


def _ab_mixer(z, B, S, conv_w, b_i, b_f, gn_ret, gn_mlstm):
    N = z.shape[1]
    z3 = z.reshape(B, S, N)
    nc = S // CHUNK
    rw = RET_HEADS * HEAD_DIM
    mw = ML_HEADS * HEAD_DIM

    pos = jnp.arange(S, dtype=F32)
    inv = 1.0 / (ROPE_BASE ** (jnp.arange(0, HEAD_DIM, 2, dtype=F32) / HEAD_DIM))
    ang = pos[:, None] * inv[None, :]
    cos = jnp.concatenate([jnp.cos(ang), jnp.cos(ang)], axis=1)
    sin = jnp.concatenate([-jnp.sin(ang), jnp.sin(ang)], axis=1)
    log_g = jnp.log(1.0 - 2.0 ** (-5.0 - jnp.arange(RET_HEADS, dtype=F32)))
    idx = jnp.arange(CHUNK, dtype=F32)
    diff = idx[:, None] - idx[None, :]
    cm = diff >= 0
    dmat = jnp.where(cm[None], jnp.exp(jnp.where(cm, diff, 0.0)[None] * log_g[:, None, None]), 0.0)
    xi = jnp.exp((idx + 1.0)[:, None] * log_g[None, :])
    zeta = jnp.exp((CHUNK - 1.0 - idx)[:, None] * log_g[None, :])
    gch = jnp.exp(CHUNK * log_g).reshape(RET_HEADS, 1, 1)
    gbias = jnp.zeros((1, LANES), F32).at[0, :ML_HEADS].set(b_i).at[0, ML_HEADS:2 * ML_HEADS].set(b_f)

    def zcol(width, blk):
        return pl.BlockSpec((None, CHUNK, width), lambda b, c: (b, c, blk))

    def full(shape):
        return pl.BlockSpec(shape, lambda b, c: (0,) * len(shape))

    return pl.pallas_call(
        _ab_mixer_kernel,
        grid=(B, nc),
        in_specs=[zcol(rw, 0), zcol(rw, 1), zcol(rw, 2), zcol(rw, 3),
                  zcol(mw, 4), zcol(mw, 5), zcol(mw, 6), zcol(mw, 7),
                  zcol(LANES, (4 * rw + 4 * mw) // LANES),
                  pl.BlockSpec((CHUNK, HEAD_DIM), lambda b, c: (c, 0)),
                  pl.BlockSpec((CHUNK, HEAD_DIM), lambda b, c: (c, 0)),
                  full((RET_HEADS, CHUNK, CHUNK)), full((CHUNK, RET_HEADS)), full((CHUNK, RET_HEADS)),
                  full((RET_HEADS, 1, 1)),
                  full((CONV_K, mw)), full((CONV_K, mw)), full((1, LANES)),
                  full((1, rw)), full((1, mw))],
        out_specs=pl.BlockSpec((None, CHUNK, rw + mw), lambda b, c: (b, c, 0)),
        out_shape=jax.ShapeDtypeStruct((B, S, rw + mw), BF16),
        scratch_shapes=[pltpu.VMEM((RET_HEADS, HEAD_DIM, HEAD_DIM), F32),
                        pltpu.VMEM((ML_HEADS, HEAD_DIM, 2 * HEAD_DIM), F32),
                        pltpu.VMEM((ML_HEADS, 1, 1), F32),
                        pltpu.VMEM((8, mw), F32),
                        pltpu.VMEM((8, mw), F32)],
        compiler_params=_params("parallel", "arbitrary"),
        name="ab_mixer",
    )(z3, z3, z3, z3, z3, z3, z3, z3, z3, cos, sin, dmat, xi, zeta, gch,
      conv_w[:, :mw], conv_w[:, mw:], gbias, gn_ret.reshape(1, rw), gn_mlstm.reshape(1, mw)
      ).reshape(B * S, rw + mw)


def _split_dot(a, b):
    a_hi = a.astype(BF16)
    b_hi = b.astype(BF16)
    a_lo = (a - a_hi.astype(F32)).astype(BF16)
    b_lo = (b - b_hi.astype(F32)).astype(BF16)
    return _dot(a_hi, b_hi) + (_dot(a_hi, b_lo) + _dot(a_lo, b_hi))


def _unit_lower_inverses(mats):
    r = lax.broadcasted_iota(jnp.int32, (CHUNK, CHUNK), 0)
    c = lax.broadcasted_iota(jnp.int32, (CHUNK, CHUNK), 1)
    eye = (r == c).astype(F32)
    ps = [-a for a in mats]
    ts = [eye + p for p in ps]
    n = 2
    while n < CHUNK:
        ps = [_split_dot(p, p) for p in ps]
        ts = [t + _split_dot(t, p) for t, p in zip(ts, ps)]
        n *= 2
    return ts


def _dn_mixer_kernel(qkv_ref, gate_ref, g_ref, conv_ref, gpar_ref, nw_ref,
                     o_ref, state_ref, tail_ref):
    c = pl.program_id(1)

    @pl.when(c == 0)
    def _():
        state_ref[...] = jnp.zeros_like(state_ref)
        tail_ref[...] = jnp.zeros_like(tail_ref)

    w = DN_HEADS * HEAD_DIM
    qkv = _causal_conv_silu(qkv_ref[...], tail_ref, conv_ref)
    gates = g_ref[...]
    beta_all = _sigmoid(gates)
    g_all = -jnp.exp(gpar_ref[1:2, :]) * _softplus(gates + gpar_ref[0:1, :])
    gcum = _dot(_tril(), g_all, precision=HIGHEST)
    gcum_t = gcum.T
    lower = _tril(None)
    strict = _tril(None, 1)
    heads = range(DN_HEADS)

    a_mats, locs = [], []
    for h in heads:
        q = qkv[:, h * HEAD_DIM:(h + 1) * HEAD_DIM]
        k = qkv[:, w + h * HEAD_DIM:w + (h + 1) * HEAD_DIM]
        v = qkv[:, 2 * w + h * HEAD_DIM:2 * w + (h + 1) * HEAD_DIM]
        q = q * lax.rsqrt(jnp.sum(q * q, axis=-1, keepdims=True) + NORM_EPS) * HEAD_DIM ** -0.5
        k = k * lax.rsqrt(jnp.sum(k * k, axis=-1, keepdims=True) + NORM_EPS)
        beta = beta_all[:, h:h + 1]
        g_col = gcum[:, DN_HEADS + h:DN_HEADS + h + 1]
        g_row = gcum_t[DN_HEADS + h:DN_HEADS + h + 1, :]
        g_last = g_col[CHUNK - 1:CHUNK, :]
        decay = jnp.exp(jnp.where(lower, g_col - g_row, NEG_INF))
        kb = k * beta
        kbf = k.astype(BF16)
        a_mats.append(jnp.where(strict, _dot_nt(kb.astype(BF16), kbf) * decay, 0.0))
        eg = jnp.exp(g_col)
        locs.append(dict(
            rhs=jnp.concatenate([v * beta, kb * eg], axis=1),
            attn=(_dot_nt(q.astype(BF16), kbf) * decay).astype(BF16),
            q_dec=(q * eg).astype(BF16),
            k_dec_t=(k * jnp.exp(g_last - g_col)).T.astype(BF16),
            g_last=jnp.exp(g_last)))

    t_invs = _unit_lower_inverses(a_mats)
    uws = [_split_dot(t, loc["rhs"]) for t, loc in zip(t_invs, locs)]

    for h, uw, loc in zip(heads, uws, locs):
        sl = slice(h * HEAD_DIM, (h + 1) * HEAD_DIM)
        u = uw[:, :HEAD_DIM]
        wm = uw[:, HEAD_DIM:]
        st = state_ref[h]
        stb = st.astype(BF16)
        vnb = (u - _dot(wm.astype(BF16), stb)).astype(BF16)
        o = _dot(loc["q_dec"], stb) + _dot(loc["attn"], vnb)
        state_ref[h] = loc["g_last"] * st + _dot(loc["k_dec_t"], vnb)
        y = _rms(o, nw_ref[...]) * _silu(gate_ref[:, sl])
        o_ref[:, sl] = y.astype(o_ref.dtype)


def _dn_mixer(z, B, S, conv_w, a_log, dt_bias, norm_w):
    N = z.shape[1]
    z3 = z.reshape(B, S, N)
    nc = S // CHUNK
    w = DN_HEADS * HEAD_DIM
    gpar = jnp.zeros((8, LANES), F32)
    gpar = gpar.at[0, DN_HEADS:2 * DN_HEADS].set(dt_bias).at[1, DN_HEADS:2 * DN_HEADS].set(a_log)

    def full(shape):
        return pl.BlockSpec(shape, lambda b, c: (0,) * len(shape))

    return pl.pallas_call(
        _dn_mixer_kernel,
        grid=(B, nc),
        in_specs=[pl.BlockSpec((None, CHUNK, 3 * w), lambda b, c: (b, c, 0)),
                  pl.BlockSpec((None, CHUNK, w), lambda b, c: (b, c, 3)),
                  pl.BlockSpec((None, CHUNK, LANES), lambda b, c: (b, c, 4 * w // LANES)),
                  full((CONV_K, 3 * w)), full((8, LANES)), full((1, HEAD_DIM))],
        out_specs=pl.BlockSpec((None, CHUNK, w), lambda b, c: (b, c, 0)),
        out_shape=jax.ShapeDtypeStruct((B, S, w), BF16),
        scratch_shapes=[pltpu.VMEM((DN_HEADS, HEAD_DIM, HEAD_DIM), F32),
                        pltpu.VMEM((8, 3 * w), F32)],
        compiler_params=_params("parallel", "arbitrary"),
        name="dn_mixer",
    )(z3, z3, z3, conv_w, gpar, norm_w.reshape(1, HEAD_DIM)).reshape(B * S, w)


def _candidate_pairs(n):
    return [(i, j) for i in range(n) for j in range(n) if (i + 1) * (j + 1) <= n]


def _peer_route_kernel(h_ref, nw_ref, wq_ref, keys_ref, xn_ref, s1_ref, s2_ref, st_ref,
                       work_ref, top_ref):
    xn = _rms(h_ref[...], nw_ref[...])
    xn_t = xn.T.astype(BF16)
    xn_ref[...] = xn_t
    q_t = _dot(wq_ref[...], xn_t)
    for h in range(PEER_HEADS):
        for p in range(2):
            lo = (2 * h + p) * PEER_NK
            s = _dot(keys_ref[p], q_t[lo:lo + PEER_NK].astype(BF16))
            (s1_ref if p == 0 else s2_ref)[h] = s
            work_ref[p, h] = s

    n_top = PEER_TOPK + 1

    def extract(r, carry):
        for p in range(2):
            for h in range(PEER_HEADS):
                cur = work_ref[p, h]
                m = jnp.max(cur, axis=0, keepdims=True)
                top_ref[p, pl.ds(r, 1), h:h + 1, :] = m[None]
                work_ref[p, h] = jnp.where(cur >= m, NEG_INF, cur)
        return carry

    lax.fori_loop(0, n_top, extract, 0)

    a = [top_ref[0, r] for r in range(n_top)]
    b = [top_ref[1, r] for r in range(n_top)]
    cands = [a[i] + b[j] for i, j in _candidate_pairs(n_top)]
    cur = list(cands)
    for _ in range(PEER_TOPK - 1):
        m = functools.reduce(jnp.maximum, cur)
        cur = [jnp.where(x >= m, NEG_INF, x) for x in cur]
    kth = functools.reduce(jnp.maximum, cur)
    cur = [jnp.where(x >= kth, NEG_INF, x) for x in cur]
    nxt = functools.reduce(jnp.maximum, cur)
    theta = jnp.where(nxt > NEG_INF, 0.5 * (kth + nxt), kth)
    top = a[0] + b[0]
    z = functools.reduce(lambda u, v: u + v,
                         [jnp.where(x >= theta, jnp.exp(x - top), 0.0) for x in cands])
    st_ref[0:8, :] = theta
    st_ref[8:16, :] = a[0]
    st_ref[16:24, :] = b[0]
    st_ref[24:32, :] = 1.0 / z


def _peer_route(h, norm_w, wq_t, keys):
    T, D = h.shape
    tt = _pick_tile(T, 256)
    nq = wq_t.shape[0]
    return pl.pallas_call(
        _peer_route_kernel,
        grid=(T // tt,),
        in_specs=[pl.BlockSpec((tt, D), lambda i: (i, 0)),
                  pl.BlockSpec((1, D), lambda i: (0, 0)),
                  pl.BlockSpec((nq, D), lambda i: (0, 0)),
                  pl.BlockSpec((2, PEER_NK, PEER_NK), lambda i: (0, 0, 0))],
        out_specs=[pl.BlockSpec((D, tt), lambda i: (0, i)),
                   pl.BlockSpec((PEER_HEADS, PEER_NK, tt), lambda i: (0, 0, i)),
                   pl.BlockSpec((PEER_HEADS, PEER_NK, tt), lambda i: (0, 0, i)),
                   pl.BlockSpec((4 * PEER_HEADS, tt), lambda i: (0, i))],
        out_shape=[jax.ShapeDtypeStruct((D, T), BF16),
                   jax.ShapeDtypeStruct((PEER_HEADS, PEER_NK, T), F32),
                   jax.ShapeDtypeStruct((PEER_HEADS, PEER_NK, T), F32),
                   jax.ShapeDtypeStruct((4 * PEER_HEADS, T), F32)],
        scratch_shapes=[pltpu.VMEM((2, PEER_HEADS, PEER_NK, tt), F32),
                        pltpu.VMEM((2, PEER_TOPK + 1, PEER_HEADS, tt), F32)],
        compiler_params=_params("parallel"),
        name="peer_route",
    )(h, norm_w.reshape(1, D), wq_t, keys)


def _peer_dense_kernel(xn_ref, u_ref, vt_ref, s1_ref, s2_ref, st_ref, h_ref, o_ref,
                       acc_ref, tau_ref, e1_ref, se_ref, act0_ref, act1_ref, wg0_ref, wg1_ref):
    j = pl.program_id(1)
    nsteps = pl.num_programs(1)
    eb, tt = wg0_ref.shape
    nblk = eb // PEER_NK
    rows_per_op = 16

    @pl.when(j == 0)
    def _():
        acc_ref[...] = jnp.zeros_like(acc_ref)
        act1_ref[...] = jnp.zeros_like(act1_ref)
        wg1_ref[...] = jnp.zeros_like(wg1_ref)
        for h in range(PEER_HEADS):
            s1 = s1_ref[h]
            tau_ref[h] = st_ref[h:h + 1, :] - s1
            e1_ref[h] = jnp.exp(s1 - st_ref[8 + h:9 + h, :]) * (0.5 * st_ref[24 + h:25 + h, :])
            s2 = s2_ref[h]
            e2 = jnp.exp(s2 - st_ref[16 + h:17 + h, :])
            for c in range(tt // LANES):
                lanes = slice(c * LANES, (c + 1) * LANES)
                se_ref[h, c, :, 0] = s2[:, lanes].reshape(PEER_NK // SUBLANES, SUBLANES, LANES)
                se_ref[h, c, :, 1] = e2[:, lanes].reshape(PEER_NK // SUBLANES, SUBLANES, LANES)

    blk = jnp.clip(j - 1, 0, nsteps - 3)

    def stages(act_w, act_r, wg_w, wg_r):
        def act_piece(m0, cols):
            act_w[m0:m0 + PEER_NK, cols] = _dot(u_ref[m0:m0 + PEER_NK, :], xn_ref[:, cols])

        def acc_piece(d0, cols):
            acc_ref[d0:d0 + MXU_N, cols] += _dot(vt_ref[d0:d0 + MXU_N, :], wg_r[:, cols])

        pieces = []
        for n0 in range(0, tt, MXU_N):
            cols = slice(n0, n0 + MXU_N)
            pieces += [functools.partial(act_piece, m0, cols) for m0 in range(0, eb, PEER_NK)]
            pieces += [functools.partial(acc_piece, d0, cols) for d0 in range(0, acc_ref.shape[0], MXU_N)]
        n_sub = nblk * (tt // LANES)
        done = 0
        for kb in range(nblk):
            i1 = blk * nblk + kb
            tau_rows = [tau_ref[h, pl.ds(i1, 1), :] for h in range(PEER_HEADS)]
            e1_rows = [e1_ref[h, pl.ds(i1, 1), :] for h in range(PEER_HEADS)]
            for c in range(tt // LANES):
                sub = kb * (tt // LANES) + c
                while done * n_sub < (sub + 1) * len(pieces):
                    pieces[done]()
                    done += 1
                lanes = slice(c * LANES, (c + 1) * LANES)
                shape = (SUBLANES, LANES)
                taus = [jnp.broadcast_to(r[:, lanes], shape) for r in tau_rows]
                e1s = [jnp.broadcast_to(r[:, lanes], shape) for r in e1_rows]
                for r0 in range(0, PEER_NK, rows_per_op):
                    halves = []
                    for r1 in range(r0, r0 + rows_per_op, SUBLANES):
                        g = r1 // SUBLANES
                        w = None
                        for h in range(PEER_HEADS):
                            t = jnp.where(se_ref[h, c, g, 0] >= taus[h], se_ref[h, c, g, 1], 0.0) * e1s[h]
                            w = t if w is None else w + t
                        a = act_r[kb * PEER_NK + r1:kb * PEER_NK + r1 + SUBLANES, lanes]
                        halves.append((w * a) * (1.0 + lax.erf(a * (1.0 / math.sqrt(2.0)))))
                    erows = slice(kb * PEER_NK + r0, kb * PEER_NK + r0 + rows_per_op)
                    wg_w[erows, lanes] = jnp.concatenate(halves, axis=0).astype(BF16)

    @pl.when(j % 2 == 0)
    def _():
        stages(act0_ref, act1_ref, wg0_ref, wg1_ref)

    @pl.when(j % 2 == 1)
    def _():
        stages(act1_ref, act0_ref, wg1_ref, wg0_ref)

    @pl.when(j == nsteps - 1)
    def _():
        o_ref[...] = h_ref[...] + acc_ref[...].T


def _peer_dense(xn_t, u, v_t, s1, s2, stats, h, *, tt_cap=512, eb=512):
    D, T = xn_t.shape
    E = u.shape[0]
    tt = _pick_tile(T, tt_cap)
    nb = E // eb
    hs = (PEER_HEADS, PEER_NK, tt)
    return pl.pallas_call(
        _peer_dense_kernel,
        grid=(T // tt, nb + 2),
        in_specs=[pl.BlockSpec((D, tt), lambda i, j: (0, i)),
                  pl.BlockSpec((eb, D), lambda i, j: (jnp.minimum(j, nb - 1), 0)),
                  pl.BlockSpec((D, eb), lambda i, j: (0, jnp.clip(j - 2, 0, nb - 1))),
                  pl.BlockSpec(hs, lambda i, j: (0, 0, i)),
                  pl.BlockSpec(hs, lambda i, j: (0, 0, i)),
                  pl.BlockSpec((4 * PEER_HEADS, tt), lambda i, j: (0, i)),
                  pl.BlockSpec((tt, D), lambda i, j: (i, 0))],
        out_specs=pl.BlockSpec((tt, D), lambda i, j: (i, 0)),
        out_shape=jax.ShapeDtypeStruct((T, D), F32),
        scratch_shapes=[pltpu.VMEM((D, tt), F32),
                        pltpu.VMEM(hs, F32), pltpu.VMEM(hs, F32),
                        pltpu.VMEM((PEER_HEADS, tt // LANES, PEER_NK // SUBLANES, 2, SUBLANES, LANES), F32),
                        pltpu.VMEM((eb, tt), F32), pltpu.VMEM((eb, tt), F32),
                        pltpu.VMEM((eb, tt), BF16), pltpu.VMEM((eb, tt), BF16)],
        compiler_params=_params("parallel", "arbitrary"),
        name="peer_dense",
    )(xn_t, u, v_t, s1, s2, stats, h)


def _ple_kernel(h_ref, p_ref, nw_ref, wg_ref, wp_ref, fw_ref, o_ref, *, final_norm):
    h = h_ref[...]
    gate = _sigmoid(_dot(_rms(h, nw_ref[...]).astype(BF16), wg_ref[...]))
    e = _dot(p_ref[...].astype(BF16), wp_ref[...])
    out = h + gate * e
    if final_norm:
        out = _rms(out, fw_ref[...])
    o_ref[...] = out


def _ple(h, p, norm_w, w_gate, w_proj, final_w, final_norm):
    T, D = h.shape
    P = p.shape[1]
    tm = _pick_tile(T, 512)
    return pl.pallas_call(
        functools.partial(_ple_kernel, final_norm=final_norm),
        grid=(T // tm,),
        in_specs=[pl.BlockSpec((tm, D), lambda i: (i, 0)),
                  pl.BlockSpec((tm, P), lambda i: (i, 0)),
                  pl.BlockSpec((1, D), lambda i: (0, 0)),
                  pl.BlockSpec((D, D), lambda i: (0, 0)),
                  pl.BlockSpec((P, D), lambda i: (0, 0)),
                  pl.BlockSpec((1, D), lambda i: (0, 0))],
        out_specs=pl.BlockSpec((tm, D), lambda i: (i, 0)),
        out_shape=jax.ShapeDtypeStruct((T, D), F32),
        compiler_params=_params("parallel"),
        name="ple",
    )(h, p, norm_w.reshape(1, D), w_gate, w_proj, final_w.reshape(1, D))


def _pad_cols(w, mult=LANES):
    n = w.shape[1]
    return jnp.pad(w, ((0, 0), (0, (-n) % mult)))


def kernel(x, p, norm_mix_w, norm_ffn_w, ab_w_in, ab_conv_w, ab_b_i, ab_b_f, ab_gn_ret, ab_gn_mlstm, ab_w_out, dn_w_in, dn_conv_w, dn_a_log, dn_dt_bias, dn_norm_w, dn_w_out, peer_w_q, peer_sub_keys, peer_u, peer_v, ple_w_proj, ple_w_gate, ple_norm_w, final_norm_w):
    B, S, D = x.shape
    depth = p.shape[0]
    T = B * S
    h = x.reshape(T, D)
    for i in range(depth):
        j = i // 2
        if i % 2 == 0:
            z = _norm_matmul(h, norm_mix_w[i], _pad_cols(ab_w_in[j]).astype(BF16))
            mixed = _ab_mixer(z, B, S, ab_conv_w[j], ab_b_i[j], ab_b_f[j], ab_gn_ret[j], ab_gn_mlstm[j])
            h = _matmul_residual(mixed, ab_w_out[j].astype(BF16), h)
        else:
            z = _norm_matmul(h, norm_mix_w[i], _pad_cols(dn_w_in[j]).astype(BF16))
            mixed = _dn_mixer(z, B, S, dn_conv_w[j], dn_a_log[j], dn_dt_bias[j], dn_norm_w[j])
            h = _matmul_residual(mixed, dn_w_out[j].astype(BF16), h)
        xn_t, s1, s2, stats = _peer_route(h, norm_ffn_w[i], peer_w_q[i].T.astype(BF16),
                                          peer_sub_keys[i].astype(BF16))
        h = _peer_dense(xn_t, peer_u[i].astype(BF16), peer_v[i].T.astype(BF16), s1, s2, stats, h)
        h = _ple(h, p[i].reshape(T, -1), ple_norm_w[i], ple_w_gate[i].astype(BF16),
                 ple_w_proj[i].astype(BF16), final_norm_w, final_norm=(i == depth - 1))
    return h.reshape(B, S, D)
```

```python
import functools
import math

import jax
import jax.numpy as jnp
from jax import lax
from jax.experimental import pallas as pl
from jax.experimental.pallas import tpu as pltpu

F32 = jnp.float32
BF16 = jnp.bfloat16

LANES = 128
SUBLANES = 8
MXU_N = 256
CHUNK = 128
HEAD_DIM = 128
CONV_K = 4
NORM_EPS = 1e-6
ROPE_BASE = 10000.0
RET_HEADS = 4
ML_HEADS = 4
DN_HEADS = 8
PEER_HEADS = 8
PEER_NK = 128
PEER_TOPK = 16
PEER_TOKEN_TILE = 512
PEER_EXPERT_BLOCK = 512
PEER_MXU_ROWS = 256
VMEM_LIMIT = 48 * 1024 * 1024

HIGHEST = lax.Precision.HIGHEST
NEG_INF = float("-inf")


def _dot(a, b, precision=None):
    return jnp.dot(a, b, preferred_element_type=F32, precision=precision)


def _dot_nt(a, b, precision=None):
    return lax.dot_general(a, b, (((1,), (1,)), ((), ())), preferred_element_type=F32,
                           precision=precision)


def _rms(x, w):
    return x * lax.rsqrt(jnp.mean(x * x, axis=-1, keepdims=True) + NORM_EPS) * w


def _sigmoid(x):
    return 1.0 / (1.0 + jnp.exp(-x))


def _silu(x):
    return x * _sigmoid(x)


def _softplus(x):
    return jnp.maximum(x, 0.0) + jnp.log(1.0 + jnp.exp(-jnp.abs(x)))


def _params(*sem, flags=None):
    return pltpu.CompilerParams(dimension_semantics=sem, vmem_limit_bytes=VMEM_LIMIT, flags=flags)


def _pick_tile(n, cap):
    t = min(n, cap)
    while n % t:
        t //= 2
    return t


def _norm_matmul_kernel(x_ref, nw_ref, w_ref, o_ref, xn_ref):
    @pl.when(pl.program_id(1) == 0)
    def _():
        xn_ref[...] = _rms(x_ref[...], nw_ref[...]).astype(BF16)

    o_ref[...] = _dot(xn_ref[...], w_ref[...])


def _norm_matmul(h, norm_w, w):
    T, D = h.shape
    N = w.shape[1]
    tm = _pick_tile(T, 512)
    tn = N
    for cand in range(N // LANES, 0, -1):
        if (N // LANES) % cand == 0 and cand * LANES <= 1536:
            tn = cand * LANES
            break
    return pl.pallas_call(
        _norm_matmul_kernel,
        grid=(T // tm, N // tn),
        in_specs=[pl.BlockSpec((tm, D), lambda i, j: (i, 0)),
                  pl.BlockSpec((1, D), lambda i, j: (0, 0)),
                  pl.BlockSpec((D, tn), lambda i, j: (0, j))],
        out_specs=pl.BlockSpec((tm, tn), lambda i, j: (i, j)),
        out_shape=jax.ShapeDtypeStruct((T, N), F32),
        scratch_shapes=[pltpu.VMEM((tm, D), BF16)],
        compiler_params=_params("parallel", "arbitrary"),
        name="norm_matmul",
    )(h, norm_w.reshape(1, D), w)


def _matmul_residual_kernel(a_ref, w_ref, h_ref, o_ref):
    o_ref[...] = h_ref[...] + _dot(a_ref[...], w_ref[...])


def _matmul_residual(a, w, h):
    T, K = a.shape
    D = w.shape[1]
    tm = _pick_tile(T, 512)
    return pl.pallas_call(
        _matmul_residual_kernel,
        grid=(T // tm,),
        in_specs=[pl.BlockSpec((tm, K), lambda i: (i, 0)),
                  pl.BlockSpec((K, D), lambda i: (0, 0)),
                  pl.BlockSpec((tm, D), lambda i: (i, 0))],
        out_specs=pl.BlockSpec((tm, D), lambda i: (i, 0)),
        out_shape=jax.ShapeDtypeStruct((T, D), F32),
        compiler_params=_params("parallel"),
        name="matmul_residual",
    )(a, w, h)


def _causal_conv_silu(x, tail_ref, w_ref):
    tail = tail_ref[...]
    row = lax.broadcasted_iota(jnp.int32, (8, 1), 0)
    y = x * w_ref[CONV_K - 1:CONV_K, :]
    for j in range(1, CONV_K):
        xr = pltpu.roll(x, j, 0)
        top = jnp.where(row < j, pltpu.roll(tail, j, 0), xr[0:8])
        xs = jnp.concatenate([top, xr[8:]], axis=0)
        y = y + xs * w_ref[CONV_K - 1 - j:CONV_K - j, :]
    tail_ref[...] = x[CHUNK - 8:CHUNK]
    return _silu(y)


def _tril(dtype=F32, k=0):
    r = lax.broadcasted_iota(jnp.int32, (CHUNK, CHUNK), 0)
    c = lax.broadcasted_iota(jnp.int32, (CHUNK, CHUNK), 1)
    return (r - c >= k) if dtype is None else (r - c >= k).astype(dtype)


def _head_layernorm(y):
    mu = jnp.mean(y, axis=-1, keepdims=True)
    yc = y - mu
    return yc * lax.rsqrt(jnp.mean(yc * yc, axis=-1, keepdims=True) + NORM_EPS)


def _ab_mixer_kernel(rq_ref, rk_ref, rv_ref, rg_ref, mq_ref, mk_ref, mv_ref, mo_ref, g_ref,
                     cos_ref, sin_ref, dmat_ref, xi_ref, zeta_ref, gch_ref,
                     convq_ref, convk_ref, gbias_ref, gnr_ref, gnm_ref,
                     o_ref, rstate_ref, mstate_ref, mmax_ref, tailq_ref, tailk_ref):
    c = pl.program_id(1)

    @pl.when(c == 0)
    def _():
        rstate_ref[...] = jnp.zeros_like(rstate_ref)
        mstate_ref[...] = jnp.zeros_like(mstate_ref)
        mmax_ref[...] = jnp.zeros_like(mmax_ref)
        tailq_ref[...] = jnp.zeros_like(tailq_ref)
        tailk_ref[...] = jnp.zeros_like(tailk_ref)

    scale = HEAD_DIM ** -0.5
    cos = cos_ref[...]
    sin = sin_ref[...]

    for h in range(RET_HEADS):
        sl = slice(h * HEAD_DIM, (h + 1) * HEAD_DIM)
        q = rq_ref[:, sl]
        k = rk_ref[:, sl]
        v = rv_ref[:, sl]
        q = q * cos + pltpu.roll(q, HEAD_DIM // 2, 1) * sin
        k = (k * cos + pltpu.roll(k, HEAD_DIM // 2, 1) * sin) * scale
        qb, kb, vb = q.astype(BF16), k.astype(BF16), v.astype(BF16)
        st = rstate_ref[h]
        s = _dot_nt(qb, kb) * dmat_ref[h]
        o = _dot(s.astype(BF16), vb) + _dot(qb, st.astype(BF16)) * xi_ref[:, h:h + 1]
        kz = (k * zeta_ref[:, h:h + 1]).T
        rstate_ref[h] = gch_ref[h] * st + _dot(kz.astype(BF16), vb)
        y = _head_layernorm(o) * gnr_ref[:, sl] * _silu(rg_ref[:, sl])
        o_ref[:, sl] = y.astype(o_ref.dtype)

    mq = _causal_conv_silu(mq_ref[...], tailq_ref, convq_ref)
    mk = _causal_conv_silu(mk_ref[...], tailk_ref, convk_ref) * scale
    gates = g_ref[...] + gbias_ref[...]
    lf = jnp.minimum(gates, 0.0) - jnp.log(1.0 + jnp.exp(-jnp.abs(gates)))
    bcum = _dot(_tril(), lf, precision=HIGHEST)
    gates_t = gates.T
    bcum_t = bcum.T
    causal = _tril(None)
    lane = lax.broadcasted_iota(jnp.int32, (CHUNK, 2 * HEAD_DIM), 1)
    for h in range(ML_HEADS):
        sl = slice(h * HEAD_DIM, (h + 1) * HEAD_DIM)
        q = mq[:, sl]
        k = mk[:, sl]
        v = mv_ref[:, sl]
        i_col = gates[:, h:h + 1]
        i_row = gates_t[h:h + 1, :]
        b_col = bcum[:, ML_HEADS + h:ML_HEADS + h + 1]
        b_row = bcum_t[ML_HEADS + h:ML_HEADS + h + 1, :]
        m_st = mmax_ref[h]
        dlog = jnp.where(causal, b_col - b_row + i_row, NEG_INF)
        inter_log = b_col + m_st
        m_t = jnp.maximum(jnp.max(dlog, axis=-1, keepdims=True), inter_log)
        dw = jnp.exp(dlog - m_t)
        inter_w = jnp.exp(inter_log - m_t)
        qb, kb = q.astype(BF16), k.astype(BF16)
        v_aug = jnp.where(lane < HEAD_DIM, jnp.concatenate([v, v], axis=1),
                          (lane == HEAD_DIM).astype(F32)).astype(BF16)
        st = mstate_ref[h]
        s = _dot_nt(qb, kb) * dw
        nd = _dot(s.astype(BF16), v_aug) + inter_w * _dot(qb, st.astype(BF16))
        num = nd[:, :HEAD_DIM]
        den = nd[:, HEAD_DIM:HEAD_DIM + 1]
        hh = num / jnp.maximum(jnp.abs(den), jnp.exp(-m_t))
        b_last = b_col[CHUNK - 1:CHUNK, :]
        w_log = b_last - b_col + i_col
        m_new = jnp.maximum(b_last + m_st, jnp.max(w_log, axis=0, keepdims=True))
        w = jnp.exp(w_log - m_new)
        dec = jnp.exp(b_last + m_st - m_new)
        kw = (k * w).T
        mstate_ref[h] = dec * st + _dot(kw.astype(BF16), v_aug)
        mmax_ref[h] = m_new
        osl = slice(RET_HEADS * HEAD_DIM + h * HEAD_DIM, RET_HEADS * HEAD_DIM + (h + 1) * HEAD_DIM)
        y = _head_layernorm(hh) * gnm_ref[:, sl] * _sigmoid(mo_ref[:, sl])
        o_ref[:, osl] = y.astype(o_ref.dtype)


def _ab_mixer(z, B, S, conv_w, b_i, b_f, gn_ret, gn_mlstm):
    N = z.shape[1]
    z3 = z.reshape(B, S, N)
    nc = S // CHUNK
    rw = RET_HEADS * HEAD_DIM
    mw = ML_HEADS * HEAD_DIM

    pos = jnp.arange(S, dtype=F32)
    inv = 1.0 / (ROPE_BASE ** (jnp.arange(0, HEAD_DIM, 2, dtype=F32) / HEAD_DIM))
    ang = pos[:, None] * inv[None, :]
    cos = jnp.concatenate([jnp.cos(ang), jnp.cos(ang)], axis=1)
    sin = jnp.concatenate([-jnp.sin(ang), jnp.sin(ang)], axis=1)
    log_g = jnp.log(1.0 - 2.0 ** (-5.0 - jnp.arange(RET_HEADS, dtype=F32)))
    idx = jnp.arange(CHUNK, dtype=F32)
    diff = idx[:, None] - idx[None, :]
    cm = diff >= 0
    dmat = jnp.where(cm[None], jnp.exp(jnp.where(cm, diff, 0.0)[None] * log_g[:, None, None]), 0.0)
    xi = jnp.exp((idx + 1.0)[:, None] * log_g[None, :])
    zeta = jnp.exp((CHUNK - 1.0 - idx)[:, None] * log_g[None, :])
    gch = jnp.exp(CHUNK * log_g).reshape(RET_HEADS, 1, 1)
    gbias = jnp.zeros((1, LANES), F32).at[0, :ML_HEADS].set(b_i).at[0, ML_HEADS:2 * ML_HEADS].set(b_f)

    def zcol(width, blk):
        return pl.BlockSpec((None, CHUNK, width), lambda b, c: (b, c, blk))

    def full(shape):
        return pl.BlockSpec(shape, lambda b, c: (0,) * len(shape))

    return pl.pallas_call(
        _ab_mixer_kernel,
        grid=(B, nc),
        in_specs=[zcol(rw, 0), zcol(rw, 1), zcol(rw, 2), zcol(rw, 3),
                  zcol(mw, 4), zcol(mw, 5), zcol(mw, 6), zcol(mw, 7),
                  zcol(LANES, (4 * rw + 4 * mw) // LANES),
                  pl.BlockSpec((CHUNK, HEAD_DIM), lambda b, c: (c, 0)),
                  pl.BlockSpec((CHUNK, HEAD_DIM), lambda b, c: (c, 0)),
                  full((RET_HEADS, CHUNK, CHUNK)), full((CHUNK, RET_HEADS)), full((CHUNK, RET_HEADS)),
                  full((RET_HEADS, 1, 1)),
                  full((CONV_K, mw)), full((CONV_K, mw)), full((1, LANES)),
                  full((1, rw)), full((1, mw))],
        out_specs=pl.BlockSpec((None, CHUNK, rw + mw), lambda b, c: (b, c, 0)),
        out_shape=jax.ShapeDtypeStruct((B, S, rw + mw), BF16),
        scratch_shapes=[pltpu.VMEM((RET_HEADS, HEAD_DIM, HEAD_DIM), F32),
                        pltpu.VMEM((ML_HEADS, HEAD_DIM, 2 * HEAD_DIM), F32),
                        pltpu.VMEM((ML_HEADS, 1, 1), F32),
                        pltpu.VMEM((8, mw), F32),
                        pltpu.VMEM((8, mw), F32)],
        compiler_params=_params("parallel", "arbitrary"),
        name="ab_mixer",
    )(z3, z3, z3, z3, z3, z3, z3, z3, z3, cos, sin, dmat, xi, zeta, gch,
      conv_w[:, :mw], conv_w[:, mw:], gbias, gn_ret.reshape(1, rw), gn_mlstm.reshape(1, mw)
      ).reshape(B * S, rw + mw)


def _split_dot(a, b):
    a_hi = a.astype(BF16)
    b_hi = b.astype(BF16)
    a_lo = (a - a_hi.astype(F32)).astype(BF16)
    b_lo = (b - b_hi.astype(F32)).astype(BF16)
    return _dot(a_hi, b_hi) + (_dot(a_hi, b_lo) + _dot(a_lo, b_hi))


def _unit_lower_inverses(mats):
    r = lax.broadcasted_iota(jnp.int32, (CHUNK, CHUNK), 0)
    c = lax.broadcasted_iota(jnp.int32, (CHUNK, CHUNK), 1)
    eye = (r == c).astype(F32)
    ps = [-a for a in mats]
    ts = [eye + p for p in ps]
    n = 2
    while n < CHUNK:
        ps = [_split_dot(p, p) for p in ps]
        ts = [t + _split_dot(t, p) for t, p in zip(ts, ps)]
        n *= 2
    return ts


def _dn_mixer_kernel(qkv_ref, gate_ref, g_ref, conv_ref, gpar_ref, nw_ref,
                     o_ref, state_ref, tail_ref):
    c = pl.program_id(1)

    @pl.when(c == 0)
    def _():
        state_ref[...] = jnp.zeros_like(state_ref)
        tail_ref[...] = jnp.zeros_like(tail_ref)

    w = DN_HEADS * HEAD_DIM
    qkv = _causal_conv_silu(qkv_ref[...], tail_ref, conv_ref)
    gates = g_ref[...]
    beta_all = _sigmoid(gates)
    g_all = -jnp.exp(gpar_ref[1:2, :]) * _softplus(gates + gpar_ref[0:1, :])
    gcum = _dot(_tril(), g_all, precision=HIGHEST)
    gcum_t = gcum.T
    lower = _tril(None)
    strict = _tril(None, 1)
    heads = range(DN_HEADS)

    a_mats, locs = [], []
    for h in heads:
        q = qkv[:, h * HEAD_DIM:(h + 1) * HEAD_DIM]
        k = qkv[:, w + h * HEAD_DIM:w + (h + 1) * HEAD_DIM]
        v = qkv[:, 2 * w + h * HEAD_DIM:2 * w + (h + 1) * HEAD_DIM]
        q = q * lax.rsqrt(jnp.sum(q * q, axis=-1, keepdims=True) + NORM_EPS) * HEAD_DIM ** -0.5
        k = k * lax.rsqrt(jnp.sum(k * k, axis=-1, keepdims=True) + NORM_EPS)
        beta = beta_all[:, h:h + 1]
        g_col = gcum[:, DN_HEADS + h:DN_HEADS + h + 1]
        g_row = gcum_t[DN_HEADS + h:DN_HEADS + h + 1, :]
        g_last = g_col[CHUNK - 1:CHUNK, :]
        decay = jnp.exp(jnp.where(lower, g_col - g_row, NEG_INF))
        kb = k * beta
        kbf = k.astype(BF16)
        a_mats.append(jnp.where(strict, _dot_nt(kb.astype(BF16), kbf) * decay, 0.0))
        eg = jnp.exp(g_col)
        locs.append(dict(
            rhs=jnp.concatenate([v * beta, kb * eg], axis=1),
            attn=(_dot_nt(q.astype(BF16), kbf) * decay).astype(BF16),
            q_dec=(q * eg).astype(BF16),
            k_dec_t=(k * jnp.exp(g_last - g_col)).T.astype(BF16),
            g_last=jnp.exp(g_last)))

    t_invs = _unit_lower_inverses(a_mats)
    uws = [_split_dot(t, loc["rhs"]) for t, loc in zip(t_invs, locs)]

    for h, uw, loc in zip(heads, uws, locs):
        sl = slice(h * HEAD_DIM, (h + 1) * HEAD_DIM)
        u = uw[:, :HEAD_DIM]
        wm = uw[:, HEAD_DIM:]
        st = state_ref[h]
        stb = st.astype(BF16)
        vnb = (u - _dot(wm.astype(BF16), stb)).astype(BF16)
        o = _dot(loc["q_dec"], stb) + _dot(loc["attn"], vnb)
        state_ref[h] = loc["g_last"] * st + _dot(loc["k_dec_t"], vnb)
        y = _rms(o, nw_ref[...]) * _silu(gate_ref[:, sl])
        o_ref[:, sl] = y.astype(o_ref.dtype)


def _dn_mixer(z, B, S, conv_w, a_log, dt_bias, norm_w):
    N = z.shape[1]
    z3 = z.reshape(B, S, N)
    nc = S // CHUNK
    w = DN_HEADS * HEAD_DIM
    gpar = jnp.zeros((8, LANES), F32)
    gpar = gpar.at[0, DN_HEADS:2 * DN_HEADS].set(dt_bias).at[1, DN_HEADS:2 * DN_HEADS].set(a_log)

    def full(shape):
        return pl.BlockSpec(shape, lambda b, c: (0,) * len(shape))

    return pl.pallas_call(
        _dn_mixer_kernel,
        grid=(B, nc),
        in_specs=[pl.BlockSpec((None, CHUNK, 3 * w), lambda b, c: (b, c, 0)),
                  pl.BlockSpec((None, CHUNK, w), lambda b, c: (b, c, 3)),
                  pl.BlockSpec((None, CHUNK, LANES), lambda b, c: (b, c, 4 * w // LANES)),
                  full((CONV_K, 3 * w)), full((8, LANES)), full((1, HEAD_DIM))],
        out_specs=pl.BlockSpec((None, CHUNK, w), lambda b, c: (b, c, 0)),
        out_shape=jax.ShapeDtypeStruct((B, S, w), BF16),
        scratch_shapes=[pltpu.VMEM((DN_HEADS, HEAD_DIM, HEAD_DIM), F32),
                        pltpu.VMEM((8, 3 * w), F32)],
        compiler_params=_params("parallel", "arbitrary"),
        name="dn_mixer",
    )(z3, z3, z3, conv_w, gpar, norm_w.reshape(1, HEAD_DIM)).reshape(B * S, w)


def _candidate_pairs(n):
    return [(i, j) for i in range(n) for j in range(n) if (i + 1) * (j + 1) <= n]


def _peer_route_kernel(h_ref, nw_ref, wq_ref, keys_ref, xn_ref, s1_ref, s2_ref, st_ref,
                       work_ref, top_ref):
    xn = _rms(h_ref[...], nw_ref[...])
    xn_t = xn.T.astype(BF16)
    xn_ref[...] = xn_t
    q_t = _dot(wq_ref[...], xn_t)
    for h in range(PEER_HEADS):
        for p in range(2):
            lo = (2 * h + p) * PEER_NK
            s = _dot(keys_ref[p], q_t[lo:lo + PEER_NK].astype(BF16))
            (s1_ref if p == 0 else s2_ref)[h] = s
            work_ref[p, h] = s

    n_top = PEER_TOPK + 1

    def extract(r, carry):
        for p in range(2):
            for h in range(PEER_HEADS):
                cur = work_ref[p, h]
                m = jnp.max(cur, axis=0, keepdims=True)
                top_ref[p, pl.ds(r, 1), h:h + 1, :] = m[None]
                work_ref[p, h] = jnp.where(cur >= m, NEG_INF, cur)
        return carry

    lax.fori_loop(0, n_top, extract, 0)

    a = [top_ref[0, r] for r in range(n_top)]
    b = [top_ref[1, r] for r in range(n_top)]
    cands = [a[i] + b[j] for i, j in _candidate_pairs(n_top)]
    cur = list(cands)
    for _ in range(PEER_TOPK - 1):
        m = functools.reduce(jnp.maximum, cur)
        cur = [jnp.where(x >= m, NEG_INF, x) for x in cur]
    kth = functools.reduce(jnp.maximum, cur)
    cur = [jnp.where(x >= kth, NEG_INF, x) for x in cur]
    nxt = functools.reduce(jnp.maximum, cur)
    theta = jnp.where(nxt > NEG_INF, 0.5 * (kth + nxt), kth)
    top = a[0] + b[0]
    z = functools.reduce(lambda u, v: u + v,
                         [jnp.where(x >= theta, jnp.exp(x - top), 0.0) for x in cands])
    st_ref[0:8, :] = theta
    st_ref[8:16, :] = a[0]
    st_ref[16:24, :] = b[0]
    st_ref[24:32, :] = 1.0 / z


def _peer_route(h, norm_w, wq_t, keys, tt):
    T, D = h.shape
    nt = T // tt
    nq = wq_t.shape[0]
    return pl.pallas_call(
        _peer_route_kernel,
        grid=(nt,),
        in_specs=[pl.BlockSpec((tt, D), lambda i: (i, 0)),
                  pl.BlockSpec((1, D), lambda i: (0, 0)),
                  pl.BlockSpec((nq, D), lambda i: (0, 0)),
                  pl.BlockSpec((2, PEER_NK, PEER_NK), lambda i: (0, 0, 0))],
        out_specs=[pl.BlockSpec((None, D, tt), lambda i: (i, 0, 0)),
                   pl.BlockSpec((None, PEER_HEADS, PEER_NK, tt), lambda i: (i, 0, 0, 0)),
                   pl.BlockSpec((None, PEER_HEADS, PEER_NK, tt), lambda i: (i, 0, 0, 0)),
                   pl.BlockSpec((None, 4 * PEER_HEADS, tt), lambda i: (i, 0, 0))],
        out_shape=[jax.ShapeDtypeStruct((nt, D, tt), BF16),
                   jax.ShapeDtypeStruct((nt, PEER_HEADS, PEER_NK, tt), F32),
                   jax.ShapeDtypeStruct((nt, PEER_HEADS, PEER_NK, tt), F32),
                   jax.ShapeDtypeStruct((nt, 4 * PEER_HEADS, tt), F32)],
        scratch_shapes=[pltpu.VMEM((2, PEER_HEADS, PEER_NK, tt), F32),
                        pltpu.VMEM((2, PEER_TOPK + 1, PEER_HEADS, tt), F32)],
        compiler_params=_params("parallel"),
        name="peer_route",
    )(h, norm_w.reshape(1, D), wq_t, keys)


def _peer_dense_kernel(xn_ref, u_ref, vt_ref, s1_ref, s2_ref, st_ref, h_ref, o_ref,
                       acc_ref, tau_ref, e1_ref, se_ref, act0_ref, act1_ref, wg0_ref, wg1_ref):
    j = pl.program_id(1)
    nsteps = pl.num_programs(1)
    eb, tt = wg0_ref.shape
    nblk = eb // PEER_NK
    rows_per_op = 16

    @pl.when(j == 0)
    def _():
        acc_ref[...] = jnp.zeros_like(acc_ref)
        act1_ref[...] = jnp.zeros_like(act1_ref)
        wg1_ref[...] = jnp.zeros_like(wg1_ref)
        for h in range(PEER_HEADS):
            s1 = s1_ref[h]
            tau_ref[h] = st_ref[h:h + 1, :] - s1
            e1_ref[h] = jnp.exp(s1 - st_ref[8 + h:9 + h, :]) * (0.5 * st_ref[24 + h:25 + h, :])
            s2 = s2_ref[h]
            e2 = jnp.exp(s2 - st_ref[16 + h:17 + h, :])
            for c in range(tt // LANES):
                lanes = slice(c * LANES, (c + 1) * LANES)
                se_ref[h, c, :, 0] = s2[:, lanes].reshape(PEER_NK // SUBLANES, SUBLANES, LANES)
                se_ref[h, c, :, 1] = e2[:, lanes].reshape(PEER_NK // SUBLANES, SUBLANES, LANES)

    blk = jnp.clip(j - 1, 0, nsteps - 3)

    def stages(act_w, act_r, wg_w, wg_r):
        def act_piece(m0, cols):
            rows = slice(m0, m0 + PEER_MXU_ROWS)
            act_w[rows, cols] = _dot(u_ref[rows, :], xn_ref[:, cols])

        def acc_piece(d0, cols):
            rows = slice(d0, d0 + PEER_MXU_ROWS)
            acc_ref[rows, cols] += _dot(vt_ref[rows, :], wg_r[:, cols])

        pieces = []
        for n0 in range(0, tt, MXU_N):
            cols = slice(n0, n0 + MXU_N)
            pieces += [functools.partial(act_piece, m0, cols) for m0 in range(0, eb, PEER_MXU_ROWS)]
            pieces += [functools.partial(acc_piece, d0, cols)
                       for d0 in range(0, acc_ref.shape[0], PEER_MXU_ROWS)]
        n_sub = nblk * (tt // LANES)
        done = 0
        for kb in range(nblk):
            i1 = blk * nblk + kb
            tau_rows = [tau_ref[h, pl.ds(i1, 1), :] for h in range(PEER_HEADS)]
            e1_rows = [e1_ref[h, pl.ds(i1, 1), :] for h in range(PEER_HEADS)]
            for c in range(tt // LANES):
                sub = kb * (tt // LANES) + c
                while done * n_sub < (sub + 1) * len(pieces):
                    pieces[done]()
                    done += 1
                lanes = slice(c * LANES, (c + 1) * LANES)
                shape = (SUBLANES, LANES)
                taus = [jnp.broadcast_to(r[:, lanes], shape) for r in tau_rows]
                e1s = [jnp.broadcast_to(r[:, lanes], shape) for r in e1_rows]
                for r0 in range(0, PEER_NK, rows_per_op):
                    halves = []
                    for r1 in range(r0, r0 + rows_per_op, SUBLANES):
                        g = r1 // SUBLANES
                        w = None
                        for h in range(PEER_HEADS):
                            t = jnp.where(se_ref[h, c, g, 0] >= taus[h], se_ref[h, c, g, 1], 0.0) * e1s[h]
                            w = t if w is None else w + t
                        a = act_r[kb * PEER_NK + r1:kb * PEER_NK + r1 + SUBLANES, lanes]
                        halves.append((w * a) * (1.0 + lax.erf(a * (1.0 / math.sqrt(2.0)))))
                    erows = slice(kb * PEER_NK + r0, kb * PEER_NK + r0 + rows_per_op)
                    wg_w[erows, lanes] = jnp.concatenate(halves, axis=0).astype(BF16)

    @pl.when(j % 2 == 0)
    def _():
        stages(act0_ref, act1_ref, wg0_ref, wg1_ref)

    @pl.when(j % 2 == 1)
    def _():
        stages(act1_ref, act0_ref, wg1_ref, wg0_ref)

    @pl.when(j == nsteps - 1)
    def _():
        o_ref[...] = h_ref[...] + acc_ref[...].T


def _peer_dense(xn_t, u, v_blocks, s1, s2, stats, h):
    nt, D, tt = xn_t.shape
    nb, _, eb = v_blocks.shape
    hs = (PEER_HEADS, PEER_NK, tt)
    return pl.pallas_call(
        _peer_dense_kernel,
        grid=(nt, nb + 2),
        in_specs=[pl.BlockSpec((None, D, tt), lambda i, j: (i, 0, 0)),
                  pl.BlockSpec((eb, D), lambda i, j: (jnp.minimum(j, nb - 1), 0)),
                  pl.BlockSpec((None, D, eb), lambda i, j: (jnp.clip(j - 2, 0, nb - 1), 0, 0)),
                  pl.BlockSpec((None,) + hs, lambda i, j: (i, 0, 0, 0)),
                  pl.BlockSpec((None,) + hs, lambda i, j: (i, 0, 0, 0)),
                  pl.BlockSpec((None, 4 * PEER_HEADS, tt), lambda i, j: (i, 0, 0)),
                  pl.BlockSpec((tt, D), lambda i, j: (i, 0))],
        out_specs=pl.BlockSpec((tt, D), lambda i, j: (i, 0)),
        out_shape=jax.ShapeDtypeStruct(h.shape, F32),
        scratch_shapes=[pltpu.VMEM((D, tt), F32),
                        pltpu.VMEM(hs, F32), pltpu.VMEM(hs, F32),
                        pltpu.VMEM((PEER_HEADS, tt // LANES, PEER_NK // SUBLANES, 2, SUBLANES, LANES), F32),
                        pltpu.VMEM((eb, tt), F32), pltpu.VMEM((eb, tt), F32),
                        pltpu.VMEM((eb, tt), BF16), pltpu.VMEM((eb, tt), BF16)],
        compiler_params=_params("parallel", "arbitrary"),
        name="peer_dense",
    )(xn_t, u, v_blocks, s1, s2, stats, h)


def _ple_kernel(h_ref, p_ref, nw_ref, wg_ref, wp_ref, fw_ref, o_ref, *, final_norm):
    h = h_ref[...]
    gate = _sigmoid(_dot(_rms(h, nw_ref[...]).astype(BF16), wg_ref[...]))
    e = _dot(p_ref[...].astype(BF16), wp_ref[...])
    out = h + gate * e
    if final_norm:
        out = _rms(out, fw_ref[...])
    o_ref[...] = out


def _ple(h, p, norm_w, w_gate, w_proj, final_w, final_norm):
    T, D = h.shape
    P = p.shape[1]
    tm = _pick_tile(T, 512)
    return pl.pallas_call(
        functools.partial(_ple_kernel, final_norm=final_norm),
        grid=(T // tm,),
        in_specs=[pl.BlockSpec((tm, D), lambda i: (i, 0)),
                  pl.BlockSpec((tm, P), lambda i: (i, 0)),
                  pl.BlockSpec((1, D), lambda i: (0, 0)),
                  pl.BlockSpec((D, D), lambda i: (0, 0)),
                  pl.BlockSpec((P, D), lambda i: (0, 0)),
                  pl.BlockSpec((1, D), lambda i: (0, 0))],
        out_specs=pl.BlockSpec((tm, D), lambda i: (i, 0)),
        out_shape=jax.ShapeDtypeStruct((T, D), F32),
        compiler_params=_params("parallel"),
        name="ple",
    )(h, p, norm_w.reshape(1, D), w_gate, w_proj, final_w.reshape(1, D))


def _pad_cols(w, mult=LANES):
    n = w.shape[1]
    return jnp.pad(w, ((0, 0), (0, (-n) % mult)))


def kernel(x, p, norm_mix_w, norm_ffn_w, ab_w_in, ab_conv_w, ab_b_i, ab_b_f, ab_gn_ret, ab_gn_mlstm, ab_w_out, dn_w_in, dn_conv_w, dn_a_log, dn_dt_bias, dn_norm_w, dn_w_out, peer_w_q, peer_sub_keys, peer_u, peer_v, ple_w_proj, ple_w_gate, ple_norm_w, final_norm_w):
    B, S, D = x.shape
    depth = p.shape[0]
    T = B * S
    h = x.reshape(T, D)
    for i in range(depth):
        j = i // 2
        if i % 2 == 0:
            z = _norm_matmul(h, norm_mix_w[i], _pad_cols(ab_w_in[j]).astype(BF16))
            mixed = _ab_mixer(z, B, S, ab_conv_w[j], ab_b_i[j], ab_b_f[j], ab_gn_ret[j], ab_gn_mlstm[j])
            h = _matmul_residual(mixed, ab_w_out[j].astype(BF16), h)
        else:
            z = _norm_matmul(h, norm_mix_w[i], _pad_cols(dn_w_in[j]).astype(BF16))
            mixed = _dn_mixer(z, B, S, dn_conv_w[j], dn_a_log[j], dn_dt_bias[j], dn_norm_w[j])
            h = _matmul_residual(mixed, dn_w_out[j].astype(BF16), h)
        xn_t, s1, s2, stats = _peer_route(h, norm_ffn_w[i], peer_w_q[i].T.astype(BF16),
                                          peer_sub_keys[i].astype(BF16), _pick_tile(T, PEER_TOKEN_TILE))
        n_exp = peer_v.shape[1]
        v_blocks = peer_v[i].astype(BF16).reshape(n_exp // PEER_EXPERT_BLOCK, PEER_EXPERT_BLOCK, D)
        h = _peer_dense(xn_t, peer_u[i].astype(BF16), v_blocks.transpose(0, 2, 1), s1, s2, stats, h)
        h = _ple(h, p[i].reshape(T, -1), ple_norm_w[i], ple_w_gate[i].astype(BF16),
                 ple_w_proj[i].astype(BF16), final_norm_w, final_norm=(i == depth - 1))
    return h.reshape(B, S, D)
```

```python
import functools
import math

import jax
import jax.numpy as jnp
from jax import lax
from jax.experimental import pallas as pl
from jax.experimental.pallas import tpu as pltpu

F32 = jnp.float32
BF16 = jnp.bfloat16

LANES = 128
SUBLANES = 8
BF16_ROWS = 16
CHUNK = 128
HEAD_DIM = 128
CONV_K = 4
NORM_EPS = 1e-6
ROPE_BASE = 10000.0
RET_HEADS = 4
ML_HEADS = 4
DN_HEADS = 8
PEER_HEADS = 8
PEER_NK = 128
PEER_TOPK = 16
PEER_TOKEN_TILE = 512
PEER_EXPERT_BLOCK = 1024
VMEM_LIMIT = 48 * 1024 * 1024

HIGHEST = lax.Precision.HIGHEST
NEG_INF = float("-inf")


def _dot(a, b, precision=None):
    return jnp.dot(a, b, preferred_element_type=F32, precision=precision)


def _dot_nt(a, b, precision=None):
    return lax.dot_general(a, b, (((1,), (1,)), ((), ())), preferred_element_type=F32,
                           precision=precision)


def _rms(x, w):
    return x * lax.rsqrt(jnp.mean(x * x, axis=-1, keepdims=True) + NORM_EPS) * w


def _sigmoid(x):
    return 1.0 / (1.0 + jnp.exp(-x))


def _silu(x):
    return x * _sigmoid(x)


def _softplus(x):
    return jnp.maximum(x, 0.0) + jnp.log(1.0 + jnp.exp(-jnp.abs(x)))


def _params(*sem, flags=None):
    return pltpu.CompilerParams(dimension_semantics=sem, vmem_limit_bytes=VMEM_LIMIT, flags=flags)


def _pick_tile(n, cap):
    t = min(n, cap)
    while n % t:
        t //= 2
    return t


def _norm_matmul_kernel(x_ref, nw_ref, w_ref, o_ref, xn_ref):
    @pl.when(pl.program_id(1) == 0)
    def _():
        xn_ref[...] = _rms(x_ref[...], nw_ref[...]).astype(BF16)

    o_ref[...] = _dot(xn_ref[...], w_ref[...])


def _norm_matmul(h, norm_w, w):
    T, D = h.shape
    N = w.shape[1]
    tm = _pick_tile(T, 512)
    tn = N
    for cand in range(N // LANES, 0, -1):
        if (N // LANES) % cand == 0 and cand * LANES <= 1536:
            tn = cand * LANES
            break
    return pl.pallas_call(
        _norm_matmul_kernel,
        grid=(T // tm, N // tn),
        in_specs=[pl.BlockSpec((tm, D), lambda i, j: (i, 0)),
                  pl.BlockSpec((1, D), lambda i, j: (0, 0)),
                  pl.BlockSpec((D, tn), lambda i, j: (0, j))],
        out_specs=pl.BlockSpec((tm, tn), lambda i, j: (i, j)),
        out_shape=jax.ShapeDtypeStruct((T, N), F32),
        scratch_shapes=[pltpu.VMEM((tm, D), BF16)],
        compiler_params=_params("parallel", "arbitrary"),
        name="norm_matmul",
    )(h, norm_w.reshape(1, D), w)


def _matmul_residual_kernel(a_ref, w_ref, h_ref, o_ref):
    o_ref[...] = h_ref[...] + _dot(a_ref[...], w_ref[...])


def _matmul_residual(a, w, h):
    T, K = a.shape
    D = w.shape[1]
    tm = _pick_tile(T, 512)
    return pl.pallas_call(
        _matmul_residual_kernel,
        grid=(T // tm,),
        in_specs=[pl.BlockSpec((tm, K), lambda i: (i, 0)),
                  pl.BlockSpec((K, D), lambda i: (0, 0)),
                  pl.BlockSpec((tm, D), lambda i: (i, 0))],
        out_specs=pl.BlockSpec((tm, D), lambda i: (i, 0)),
        out_shape=jax.ShapeDtypeStruct((T, D), F32),
        compiler_params=_params("parallel"),
        name="matmul_residual",
    )(a, w, h)


def _causal_conv_silu(x, tail_ref, w_ref):
    tail = tail_ref[...]
    row = lax.broadcasted_iota(jnp.int32, (8, 1), 0)
    y = x * w_ref[CONV_K - 1:CONV_K, :]
    for j in range(1, CONV_K):
        xr = pltpu.roll(x, j, 0)
        top = jnp.where(row < j, pltpu.roll(tail, j, 0), xr[0:8])
        xs = jnp.concatenate([top, xr[8:]], axis=0)
        y = y + xs * w_ref[CONV_K - 1 - j:CONV_K - j, :]
    tail_ref[...] = x[CHUNK - 8:CHUNK]
    return _silu(y)


def _tril(dtype=F32, k=0):
    r = lax.broadcasted_iota(jnp.int32, (CHUNK, CHUNK), 0)
    c = lax.broadcasted_iota(jnp.int32, (CHUNK, CHUNK), 1)
    return (r - c >= k) if dtype is None else (r - c >= k).astype(dtype)


def _head_layernorm(y):
    mu = jnp.mean(y, axis=-1, keepdims=True)
    yc = y - mu
    return yc * lax.rsqrt(jnp.mean(yc * yc, axis=-1, keepdims=True) + NORM_EPS)


def _ab_mixer_kernel(rq_ref, rk_ref, rv_ref, rg_ref, mq_ref, mk_ref, mv_ref, mo_ref, g_ref,
                     cos_ref, sin_ref, dmat_ref, xi_ref, zeta_ref, gch_ref,
                     convq_ref, convk_ref, gbias_ref, gnr_ref, gnm_ref,
                     o_ref, rstate_ref, mstate_ref, mmax_ref, tailq_ref, tailk_ref):
    c = pl.program_id(1)

    @pl.when(c == 0)
    def _():
        rstate_ref[...] = jnp.zeros_like(rstate_ref)
        mstate_ref[...] = jnp.zeros_like(mstate_ref)
        mmax_ref[...] = jnp.zeros_like(mmax_ref)
        tailq_ref[...] = jnp.zeros_like(tailq_ref)
        tailk_ref[...] = jnp.zeros_like(tailk_ref)

    scale = HEAD_DIM ** -0.5
    cos = cos_ref[...]
    sin = sin_ref[...]

    for h in range(RET_HEADS):
        sl = slice(h * HEAD_DIM, (h + 1) * HEAD_DIM)
        q = rq_ref[:, sl]
        k = rk_ref[:, sl]
        v = rv_ref[:, sl]
        q = q * cos + pltpu.roll(q, HEAD_DIM // 2, 1) * sin
        k = (k * cos + pltpu.roll(k, HEAD_DIM // 2, 1) * sin) * scale
        qb, kb, vb = q.astype(BF16), k.astype(BF16), v.astype(BF16)
        st = rstate_ref[h]
        s = _dot_nt(qb, kb) * dmat_ref[h]
        o = _dot(s.astype(BF16), vb) + _dot(qb, st.astype(BF16)) * xi_ref[:, h:h + 1]
        kz = (k * zeta_ref[:, h:h + 1]).T
        rstate_ref[h] = gch_ref[h] * st + _dot(kz.astype(BF16), vb)
        y = _head_layernorm(o) * gnr_ref[:, sl] * _silu(rg_ref[:, sl])
        o_ref[:, sl] = y.astype(o_ref.dtype)

    mq = _causal_conv_silu(mq_ref[...], tailq_ref, convq_ref)
    mk = _causal_conv_silu(mk_ref[...], tailk_ref, convk_ref) * scale
    gates = g_ref[...] + gbias_ref[...]
    lf = jnp.minimum(gates, 0.0) - jnp.log(1.0 + jnp.exp(-jnp.abs(gates)))
    bcum = _dot(_tril(), lf, precision=HIGHEST)
    gates_t = gates.T
    bcum_t = bcum.T
    causal = _tril(None)
    lane = lax.broadcasted_iota(jnp.int32, (CHUNK, 2 * HEAD_DIM), 1)
    for h in range(ML_HEADS):
        sl = slice(h * HEAD_DIM, (h + 1) * HEAD_DIM)
        q = mq[:, sl]
        k = mk[:, sl]
        v = mv_ref[:, sl]
        i_col = gates[:, h:h + 1]
        i_row = gates_t[h:h + 1, :]
        b_col = bcum[:, ML_HEADS + h:ML_HEADS + h + 1]
        b_row = bcum_t[ML_HEADS + h:ML_HEADS + h + 1, :]
        m_st = mmax_ref[h]
        dlog = jnp.where(causal, b_col - b_row + i_row, NEG_INF)
        inter_log = b_col + m_st
        m_t = jnp.maximum(jnp.max(dlog, axis=-1, keepdims=True), inter_log)
        dw = jnp.exp(dlog - m_t)
        inter_w = jnp.exp(inter_log - m_t)
        qb, kb = q.astype(BF16), k.astype(BF16)
        v_aug = jnp.where(lane < HEAD_DIM, jnp.concatenate([v, v], axis=1),
                          (lane == HEAD_DIM).astype(F32)).astype(BF16)
        st = mstate_ref[h]
        s = _dot_nt(qb, kb) * dw
        nd = _dot(s.astype(BF16), v_aug) + inter_w * _dot(qb, st.astype(BF16))
        num = nd[:, :HEAD_DIM]
        den = nd[:, HEAD_DIM:HEAD_DIM + 1]
        hh = num / jnp.maximum(jnp.abs(den), jnp.exp(-m_t))
        b_last = b_col[CHUNK - 1:CHUNK, :]
        w_log = b_last - b_col + i_col
        m_new = jnp.maximum(b_last + m_st, jnp.max(w_log, axis=0, keepdims=True))
        w = jnp.exp(w_log - m_new)
        dec = jnp.exp(b_last + m_st - m_new)
        kw = (k * w).T
        mstate_ref[h] = dec * st + _dot(kw.astype(BF16), v_aug)
        mmax_ref[h] = m_new
        osl = slice(RET_HEADS * HEAD_DIM + h * HEAD_DIM, RET_HEADS * HEAD_DIM + (h + 1) * HEAD_DIM)
        y = _head_layernorm(hh) * gnm_ref[:, sl] * _sigmoid(mo_ref[:, sl])
        o_ref[:, osl] = y.astype(o_ref.dtype)


def _ab_mixer(z, B, S, conv_w, b_i, b_f, gn_ret, gn_mlstm):
    N = z.shape[1]
    z3 = z.reshape(B, S, N)
    nc = S // CHUNK
    rw = RET_HEADS * HEAD_DIM
    mw = ML_HEADS * HEAD_DIM

    pos = jnp.arange(S, dtype=F32)
    inv = 1.0 / (ROPE_BASE ** (jnp.arange(0, HEAD_DIM, 2, dtype=F32) / HEAD_DIM))
    ang = pos[:, None] * inv[None, :]
    cos = jnp.concatenate([jnp.cos(ang), jnp.cos(ang)], axis=1)
    sin = jnp.concatenate([-jnp.sin(ang), jnp.sin(ang)], axis=1)
    log_g = jnp.log(1.0 - 2.0 ** (-5.0 - jnp.arange(RET_HEADS, dtype=F32)))
    idx = jnp.arange(CHUNK, dtype=F32)
    diff = idx[:, None] - idx[None, :]
    cm = diff >= 0
    dmat = jnp.where(cm[None], jnp.exp(jnp.where(cm, diff, 0.0)[None] * log_g[:, None, None]), 0.0)
    xi = jnp.exp((idx + 1.0)[:, None] * log_g[None, :])
    zeta = jnp.exp((CHUNK - 1.0 - idx)[:, None] * log_g[None, :])
    gch = jnp.exp(CHUNK * log_g).reshape(RET_HEADS, 1, 1)
    gbias = jnp.zeros((1, LANES), F32).at[0, :ML_HEADS].set(b_i).at[0, ML_HEADS:2 * ML_HEADS].set(b_f)

    def zcol(width, blk):
        return pl.BlockSpec((None, CHUNK, width), lambda b, c: (b, c, blk))

    def full(shape):
        return pl.BlockSpec(shape, lambda b, c: (0,) * len(shape))

    return pl.pallas_call(
        _ab_mixer_kernel,
        grid=(B, nc),
        in_specs=[zcol(rw, 0), zcol(rw, 1), zcol(rw, 2), zcol(rw, 3),
                  zcol(mw, 4), zcol(mw, 5), zcol(mw, 6), zcol(mw, 7),
                  zcol(LANES, (4 * rw + 4 * mw) // LANES),
                  pl.BlockSpec((CHUNK, HEAD_DIM), lambda b, c: (c, 0)),
                  pl.BlockSpec((CHUNK, HEAD_DIM), lambda b, c: (c, 0)),
                  full((RET_HEADS, CHUNK, CHUNK)), full((CHUNK, RET_HEADS)), full((CHUNK, RET_HEADS)),
                  full((RET_HEADS, 1, 1)),
                  full((CONV_K, mw)), full((CONV_K, mw)), full((1, LANES)),
                  full((1, rw)), full((1, mw))],
        out_specs=pl.BlockSpec((None, CHUNK, rw + mw), lambda b, c: (b, c, 0)),
        out_shape=jax.ShapeDtypeStruct((B, S, rw + mw), BF16),
        scratch_shapes=[pltpu.VMEM((RET_HEADS, HEAD_DIM, HEAD_DIM), F32),
                        pltpu.VMEM((ML_HEADS, HEAD_DIM, 2 * HEAD_DIM), F32),
                        pltpu.VMEM((ML_HEADS, 1, 1), F32),
                        pltpu.VMEM((8, mw), F32),
                        pltpu.VMEM((8, mw), F32)],
        compiler_params=_params("parallel", "arbitrary"),
        name="ab_mixer",
    )(z3, z3, z3, z3, z3, z3, z3, z3, z3, cos, sin, dmat, xi, zeta, gch,
      conv_w[:, :mw], conv_w[:, mw:], gbias, gn_ret.reshape(1, rw), gn_mlstm.reshape(1, mw)
      ).reshape(B * S, rw + mw)


def _split_dot(a, b):
    a_hi = a.astype(BF16)
    b_hi = b.astype(BF16)
    a_lo = (a - a_hi.astype(F32)).astype(BF16)
    b_lo = (b - b_hi.astype(F32)).astype(BF16)
    return _dot(a_hi, b_hi) + (_dot(a_hi, b_lo) + _dot(a_lo, b_hi))


def _unit_lower_inverses(mats):
    r = lax.broadcasted_iota(jnp.int32, (CHUNK, CHUNK), 0)
    c = lax.broadcasted_iota(jnp.int32, (CHUNK, CHUNK), 1)
    eye = (r == c).astype(F32)
    ps = [-a for a in mats]
    ts = [eye + p for p in ps]
    n = 2
    while n < CHUNK:
        ps = [_split_dot(p, p) for p in ps]
        ts = [t + _split_dot(t, p) for t, p in zip(ts, ps)]
        n *= 2
    return ts


def _dn_mixer_kernel(qkv_ref, gate_ref, g_ref, conv_ref, gpar_ref, nw_ref,
                     o_ref, state_ref, tail_ref):
    c = pl.program_id(1)

    @pl.when(c == 0)
    def _():
        state_ref[...] = jnp.zeros_like(state_ref)
        tail_ref[...] = jnp.zeros_like(tail_ref)

    w = DN_HEADS * HEAD_DIM
    qkv = _causal_conv_silu(qkv_ref[...], tail_ref, conv_ref)
    gates = g_ref[...]
    beta_all = _sigmoid(gates)
    g_all = -jnp.exp(gpar_ref[1:2, :]) * _softplus(gates + gpar_ref[0:1, :])
    gcum = _dot(_tril(), g_all, precision=HIGHEST)
    gcum_t = gcum.T
    lower = _tril(None)
    strict = _tril(None, 1)
    heads = range(DN_HEADS)

    a_mats, locs = [], []
    for h in heads:
        q = qkv[:, h * HEAD_DIM:(h + 1) * HEAD_DIM]
        k = qkv[:, w + h * HEAD_DIM:w + (h + 1) * HEAD_DIM]
        v = qkv[:, 2 * w + h * HEAD_DIM:2 * w + (h + 1) * HEAD_DIM]
        q = q * lax.rsqrt(jnp.sum(q * q, axis=-1, keepdims=True) + NORM_EPS) * HEAD_DIM ** -0.5
        k = k * lax.rsqrt(jnp.sum(k * k, axis=-1, keepdims=True) + NORM_EPS)
        beta = beta_all[:, h:h + 1]
        g_col = gcum[:, DN_HEADS + h:DN_HEADS + h + 1]
        g_row = gcum_t[DN_HEADS + h:DN_HEADS + h + 1, :]
        g_last = g_col[CHUNK - 1:CHUNK, :]
        decay = jnp.exp(jnp.where(lower, g_col - g_row, NEG_INF))
        kb = k * beta
        kbf = k.astype(BF16)
        a_mats.append(jnp.where(strict, _dot_nt(kb.astype(BF16), kbf) * decay, 0.0))
        eg = jnp.exp(g_col)
        locs.append(dict(
            rhs=jnp.concatenate([v * beta, kb * eg], axis=1),
            attn=(_dot_nt(q.astype(BF16), kbf) * decay).astype(BF16),
            q_dec=(q * eg).astype(BF16),
            k_dec_t=(k * jnp.exp(g_last - g_col)).T.astype(BF16),
            g_last=jnp.exp(g_last)))

    t_invs = _unit_lower_inverses(a_mats)
    uws = [_split_dot(t, loc["rhs"]) for t, loc in zip(t_invs, locs)]

    for h, uw, loc in zip(heads, uws, locs):
        sl = slice(h * HEAD_DIM, (h + 1) * HEAD_DIM)
        u = uw[:, :HEAD_DIM]
        wm = uw[:, HEAD_DIM:]
        st = state_ref[h]
        stb = st.astype(BF16)
        vnb = (u - _dot(wm.astype(BF16), stb)).astype(BF16)
        o = _dot(loc["q_dec"], stb) + _dot(loc["attn"], vnb)
        state_ref[h] = loc["g_last"] * st + _dot(loc["k_dec_t"], vnb)
        y = _rms(o, nw_ref[...]) * _silu(gate_ref[:, sl])
        o_ref[:, sl] = y.astype(o_ref.dtype)


def _dn_mixer(z, B, S, conv_w, a_log, dt_bias, norm_w):
    N = z.shape[1]
    z3 = z.reshape(B, S, N)
    nc = S // CHUNK
    w = DN_HEADS * HEAD_DIM
    gpar = jnp.zeros((8, LANES), F32)
    gpar = gpar.at[0, DN_HEADS:2 * DN_HEADS].set(dt_bias).at[1, DN_HEADS:2 * DN_HEADS].set(a_log)

    def full(shape):
        return pl.BlockSpec(shape, lambda b, c: (0,) * len(shape))

    return pl.pallas_call(
        _dn_mixer_kernel,
        grid=(B, nc),
        in_specs=[pl.BlockSpec((None, CHUNK, 3 * w), lambda b, c: (b, c, 0)),
                  pl.BlockSpec((None, CHUNK, w), lambda b, c: (b, c, 3)),
                  pl.BlockSpec((None, CHUNK, LANES), lambda b, c: (b, c, 4 * w // LANES)),
                  full((CONV_K, 3 * w)), full((8, LANES)), full((1, HEAD_DIM))],
        out_specs=pl.BlockSpec((None, CHUNK, w), lambda b, c: (b, c, 0)),
        out_shape=jax.ShapeDtypeStruct((B, S, w), BF16),
        scratch_shapes=[pltpu.VMEM((DN_HEADS, HEAD_DIM, HEAD_DIM), F32),
                        pltpu.VMEM((8, 3 * w), F32)],
        compiler_params=_params("parallel", "arbitrary"),
        name="dn_mixer",
    )(z3, z3, z3, conv_w, gpar, norm_w.reshape(1, HEAD_DIM)).reshape(B * S, w)


def _pack_rows(x):
    return pltpu.bitcast(x.astype(BF16), jnp.uint32)


def _candidate_pairs(n):
    return [(i, j) for i in range(n) for j in range(n) if (i + 1) * (j + 1) <= n]


def _peer_route_kernel(h_ref, nw_ref, wq_ref, keys_ref, xn_ref, cnt_ref, e1_ref, qe_ref,
                       s_ref, work_ref, top_ref, st_ref, rank_ref):
    tt = xn_ref.shape[1]
    xn = _rms(h_ref[...], nw_ref[...])
    xn_t = xn.T.astype(BF16)
    xn_ref[...] = xn_t
    q_t = _dot(wq_ref[...], xn_t)
    for h in range(PEER_HEADS):
        for p in range(2):
            lo = (2 * h + p) * PEER_NK
            s = _dot(keys_ref[p], q_t[lo:lo + PEER_NK].astype(BF16))
            s_ref[p, h] = s
            work_ref[p, h] = s

    n_top = PEER_TOPK + 1

    rank_ref[...] = jnp.full(rank_ref.shape, float(PEER_TOPK), F32)

    def extract(r, carry):
        for p in range(2):
            for h in range(PEER_HEADS):
                cur = work_ref[p, h]
                m = jnp.max(cur, axis=0, keepdims=True)
                top_ref[p, pl.ds(r, 1), h:h + 1, :] = m[None]
                hit = cur >= m
                work_ref[p, h] = jnp.where(hit, NEG_INF, cur)
                if p == 1:
                    rank_ref[h] = jnp.where(hit, jnp.minimum(r, PEER_TOPK).astype(F32), rank_ref[h])
        return carry

    lax.fori_loop(0, n_top, extract, 0)

    a = [top_ref[0, r] for r in range(n_top)]
    b = [top_ref[1, r] for r in range(n_top)]
    cands = [a[i] + b[j] for i, j in _candidate_pairs(n_top)]
    cur = list(cands)
    for _ in range(PEER_TOPK - 1):
        m = functools.reduce(jnp.maximum, cur)
        cur = [jnp.where(x >= m, NEG_INF, x) for x in cur]
    kth = functools.reduce(jnp.maximum, cur)
    cur = [jnp.where(x >= kth, NEG_INF, x) for x in cur]
    nxt = functools.reduce(jnp.maximum, cur)
    theta = jnp.where(nxt > NEG_INF, 0.5 * (kth + nxt), kth)
    top = a[0] + b[0]
    z = functools.reduce(lambda u, v: u + v,
                         [jnp.where(x >= theta, jnp.exp(x - top), 0.0) for x in cands])
    st_ref[0:8, :] = theta
    st_ref[8:16, :] = a[0]
    st_ref[16:24, :] = b[0]
    st_ref[24:32, :] = 0.5 / z

    groups = PEER_NK // BF16_ROWS
    for h in range(PEER_HEADS):
        s1 = s_ref[0, h]
        s2 = s_ref[1, h]
        tau = st_ref[h:h + 1, :] - s1
        cnt = jnp.zeros_like(s1)
        for r in range(PEER_TOPK):
            cnt = cnt + jnp.where(top_ref[1, r, h:h + 1, :] >= tau, 1.0, 0.0)
        cnt_ref[h] = cnt
        rank = rank_ref[h]
        e1_ref[h] = jnp.exp(s1 - st_ref[8 + h:9 + h, :]) * st_ref[24 + h:25 + h, :]
        e2 = jnp.exp(s2 - st_ref[16 + h:17 + h, :])
        for c in range(tt // LANES):
            lanes = slice(c * LANES, (c + 1) * LANES)
            qe_ref[h, c, :, 0] = _pack_rows(rank[:, lanes]).reshape(groups, SUBLANES, LANES)
            qe_ref[h, c, :, 1] = _pack_rows(e2[:, lanes]).reshape(groups, SUBLANES, LANES)


def _peer_route(h, norm_w, wq_t, keys, tt):
    T, D = h.shape
    nt = T // tt
    nq = wq_t.shape[0]
    hs = (PEER_HEADS, PEER_NK, tt)
    qs = (PEER_HEADS, tt // LANES, PEER_NK // BF16_ROWS, 2, SUBLANES, LANES)
    return pl.pallas_call(
        _peer_route_kernel,
        grid=(nt,),
        in_specs=[pl.BlockSpec((tt, D), lambda i: (i, 0)),
                  pl.BlockSpec((1, D), lambda i: (0, 0)),
                  pl.BlockSpec((nq, D), lambda i: (0, 0)),
                  pl.BlockSpec((2, PEER_NK, PEER_NK), lambda i: (0, 0, 0))],
        out_specs=[pl.BlockSpec((None, D, tt), lambda i: (i, 0, 0)),
                   pl.BlockSpec((None,) + hs, lambda i: (i, 0, 0, 0)),
                   pl.BlockSpec((None,) + hs, lambda i: (i, 0, 0, 0)),
                   pl.BlockSpec((None,) + qs, lambda i: (i, 0, 0, 0, 0, 0, 0))],
        out_shape=[jax.ShapeDtypeStruct((nt, D, tt), BF16),
                   jax.ShapeDtypeStruct((nt,) + hs, F32),
                   jax.ShapeDtypeStruct((nt,) + hs, F32),
                   jax.ShapeDtypeStruct((nt,) + qs, jnp.uint32)],
        scratch_shapes=[pltpu.VMEM((2,) + hs, F32),
                        pltpu.VMEM((2,) + hs, F32),
                        pltpu.VMEM((2, PEER_TOPK + 1, PEER_HEADS, tt), F32),
                        pltpu.VMEM((4 * PEER_HEADS, tt), F32),
                        pltpu.VMEM(hs, F32)],
        compiler_params=_params("parallel"),
        name="peer_route",
    )(h, norm_w.reshape(1, D), wq_t, keys)


def _peer_dense_kernel(xn_ref, u_ref, vt_ref, cnt_ref, e1_ref, qe_ref, h_ref, o_ref,
                       acc_ref, act_ref, wg_ref):
    j = pl.program_id(1)
    eb, tt = wg_ref.shape
    nblk = eb // PEER_NK

    @pl.when(j == 0)
    def _():
        acc_ref[...] = jnp.zeros_like(acc_ref)

    act_ref[...] = _dot(u_ref[...], xn_ref[...])
    for kb in range(nblk):
        i1 = j * nblk + kb
        cnt_rows = [cnt_ref[h, pl.ds(i1, 1), :] for h in range(PEER_HEADS)]
        e1_rows = [e1_ref[h, pl.ds(i1, 1), :] for h in range(PEER_HEADS)]
        for c in range(tt // LANES):
            lanes = slice(c * LANES, (c + 1) * LANES)
            shape = (BF16_ROWS, LANES)
            cnts = [jnp.broadcast_to(r[:, lanes], shape).astype(BF16) for r in cnt_rows]
            e1s = [jnp.broadcast_to(r[:, lanes], shape).astype(BF16) for r in e1_rows]
            for g in range(PEER_NK // BF16_ROWS):
                w = None
                for h in range(PEER_HEADS):
                    rank = pltpu.bitcast(qe_ref[h, c, g, 0], BF16)
                    e2 = pltpu.bitcast(qe_ref[h, c, g, 1], BF16)
                    t = jnp.where(rank < cnts[h], e2, 0.0) * e1s[h]
                    w = t if w is None else w + t
                rows = slice(kb * PEER_NK + g * BF16_ROWS, kb * PEER_NK + (g + 1) * BF16_ROWS)
                a = act_ref[rows, lanes]
                gated = (w.astype(F32) * a) * (1.0 + lax.erf(a * (1.0 / math.sqrt(2.0))))
                wg_ref[rows, lanes] = gated.astype(BF16)
    acc_ref[...] += _dot(vt_ref[...], wg_ref[...])

    @pl.when(j == pl.num_programs(1) - 1)
    def _():
        o_ref[...] = h_ref[...] + acc_ref[...].T


def _peer_dense(xn_t, u, v_blocks, cnt, e1, qe, h):
    nt, D, tt = xn_t.shape
    nb, _, eb = v_blocks.shape
    hs = (PEER_HEADS, PEER_NK, tt)
    qs = qe.shape[1:]
    return pl.pallas_call(
        _peer_dense_kernel,
        grid=(nt, nb),
        in_specs=[pl.BlockSpec((None, D, tt), lambda i, j: (i, 0, 0)),
                  pl.BlockSpec((eb, D), lambda i, j: (j, 0)),
                  pl.BlockSpec((None, D, eb), lambda i, j: (j, 0, 0)),
                  pl.BlockSpec((None,) + hs, lambda i, j: (i, 0, 0, 0)),
                  pl.BlockSpec((None,) + hs, lambda i, j: (i, 0, 0, 0)),
                  pl.BlockSpec((None,) + qs, lambda i, j: (i, 0, 0, 0, 0, 0, 0)),
                  pl.BlockSpec((tt, D), lambda i, j: (i, 0))],
        out_specs=pl.BlockSpec((tt, D), lambda i, j: (i, 0)),
        out_shape=jax.ShapeDtypeStruct(h.shape, F32),
        scratch_shapes=[pltpu.VMEM((D, tt), F32),
                        pltpu.VMEM((eb, tt), F32),
                        pltpu.VMEM((eb, tt), BF16)],
        compiler_params=_params("parallel", "arbitrary"),
        name="peer_dense",
    )(xn_t, u, v_blocks, cnt, e1, qe, h)


def _ple_kernel(h_ref, p_ref, nw_ref, wg_ref, wp_ref, fw_ref, o_ref, *, final_norm):
    h = h_ref[...]
    gate = _sigmoid(_dot(_rms(h, nw_ref[...]).astype(BF16), wg_ref[...]))
    e = _dot(p_ref[...].astype(BF16), wp_ref[...])
    out = h + gate * e
    if final_norm:
        out = _rms(out, fw_ref[...])
    o_ref[...] = out


def _ple(h, p, norm_w, w_gate, w_proj, final_w, final_norm):
    T, D = h.shape
    P = p.shape[1]
    tm = _pick_tile(T, 512)
    return pl.pallas_call(
        functools.partial(_ple_kernel, final_norm=final_norm),
        grid=(T // tm,),
        in_specs=[pl.BlockSpec((tm, D), lambda i: (i, 0)),
                  pl.BlockSpec((tm, P), lambda i: (i, 0)),
                  pl.BlockSpec((1, D), lambda i: (0, 0)),
                  pl.BlockSpec((D, D), lambda i: (0, 0)),
                  pl.BlockSpec((P, D), lambda i: (0, 0)),
                  pl.BlockSpec((1, D), lambda i: (0, 0))],
        out_specs=pl.BlockSpec((tm, D), lambda i: (i, 0)),
        out_shape=jax.ShapeDtypeStruct((T, D), F32),
        compiler_params=_params("parallel"),
        name="ple",
    )(h, p, norm_w.reshape(1, D), w_gate, w_proj, final_w.reshape(1, D))


def _pad_cols(w, mult=LANES):
    n = w.shape[1]
    return jnp.pad(w, ((0, 0), (0, (-n) % mult)))


def kernel(x, p, norm_mix_w, norm_ffn_w, ab_w_in, ab_conv_w, ab_b_i, ab_b_f, ab_gn_ret, ab_gn_mlstm, ab_w_out, dn_w_in, dn_conv_w, dn_a_log, dn_dt_bias, dn_norm_w, dn_w_out, peer_w_q, peer_sub_keys, peer_u, peer_v, ple_w_proj, ple_w_gate, ple_norm_w, final_norm_w):
    B, S, D = x.shape
    depth = p.shape[0]
    T = B * S
    h = x.reshape(T, D)
    for i in range(depth):
        j = i // 2
        if i % 2 == 0:
            z = _norm_matmul(h, norm_mix_w[i], _pad_cols(ab_w_in[j]).astype(BF16))
            mixed = _ab_mixer(z, B, S, ab_conv_w[j], ab_b_i[j], ab_b_f[j], ab_gn_ret[j], ab_gn_mlstm[j])
            h = _matmul_residual(mixed, ab_w_out[j].astype(BF16), h)
        else:
            z = _norm_matmul(h, norm_mix_w[i], _pad_cols(dn_w_in[j]).astype(BF16))
            mixed = _dn_mixer(z, B, S, dn_conv_w[j], dn_a_log[j], dn_dt_bias[j], dn_norm_w[j])
            h = _matmul_residual(mixed, dn_w_out[j].astype(BF16), h)
        xn_t, cnt, e1, qe = _peer_route(h, norm_ffn_w[i], peer_w_q[i].T.astype(BF16),
                                        peer_sub_keys[i].astype(BF16), _pick_tile(T, PEER_TOKEN_TILE))
        n_exp = peer_v.shape[1]
        v_blocks = peer_v[i].astype(BF16).reshape(n_exp // PEER_EXPERT_BLOCK, PEER_EXPERT_BLOCK, D)
        h = _peer_dense(xn_t, peer_u[i].astype(BF16), v_blocks.transpose(0, 2, 1), cnt, e1, qe, h)
        h = _ple(h, p[i].reshape(T, -1), ple_norm_w[i], ple_w_gate[i].astype(BF16),
                 ple_w_proj[i].astype(BF16), final_norm_w, final_norm=(i == depth - 1))
    return h.reshape(B, S, D)
```

```python
import functools
import math

import jax
import jax.numpy as jnp
from jax import lax
from jax.experimental import pallas as pl
from jax.experimental.pallas import tpu as pltpu

F32 = jnp.float32
BF16 = jnp.bfloat16

LANES = 128
SUBLANES = 8
BF16_ROWS = 16
CHUNK = 128
HEAD_DIM = 128
CONV_K = 4
NORM_EPS = 1e-6
ROPE_BASE = 10000.0
RET_HEADS = 4
ML_HEADS = 4
DN_HEADS = 8
PEER_HEADS = 8
PEER_NK = 128
PEER_TOPK = 16
PEER_TOKEN_TILE = 512
PEER_EXPERT_BLOCK = 1024
VMEM_LIMIT = 48 * 1024 * 1024

HIGHEST = lax.Precision.HIGHEST
NEG_INF = float("-inf")


def _dot(a, b, precision=None):
    return jnp.dot(a, b, preferred_element_type=F32, precision=precision)


def _dot_nt(a, b, precision=None):
    return lax.dot_general(a, b, (((1,), (1,)), ((), ())), preferred_element_type=F32,
                           precision=precision)


def _rms(x, w):
    return x * lax.rsqrt(jnp.mean(x * x, axis=-1, keepdims=True) + NORM_EPS) * w


def _sigmoid(x):
    return 1.0 / (1.0 + jnp.exp(-x))


def _silu(x):
    return x * _sigmoid(x)


def _softplus(x):
    return jnp.maximum(x, 0.0) + jnp.log(1.0 + jnp.exp(-jnp.abs(x)))


def _params(*sem, flags=None):
    return pltpu.CompilerParams(dimension_semantics=sem, vmem_limit_bytes=VMEM_LIMIT, flags=flags)


def _pick_tile(n, cap):
    t = min(n, cap)
    while n % t:
        t //= 2
    return t


def _norm_matmul_kernel(x_ref, nw_ref, w_ref, o_ref, xn_ref):
    @pl.when(pl.program_id(1) == 0)
    def _():
        xn_ref[...] = _rms(x_ref[...], nw_ref[...]).astype(BF16)

    o_ref[...] = _dot(xn_ref[...], w_ref[...])


def _norm_matmul(h, norm_w, w):
    T, D = h.shape
    N = w.shape[1]
    tm = _pick_tile(T, 1024)
    tn = N
    for cand in range(N // LANES, 0, -1):
        if (N // LANES) % cand == 0 and cand * LANES <= 1536:
            tn = cand * LANES
            break
    return pl.pallas_call(
        _norm_matmul_kernel,
        grid=(T // tm, N // tn),
        in_specs=[pl.BlockSpec((tm, D), lambda i, j: (i, 0)),
                  pl.BlockSpec((1, D), lambda i, j: (0, 0)),
                  pl.BlockSpec((D, tn), lambda i, j: (0, j))],
        out_specs=pl.BlockSpec((tm, tn), lambda i, j: (i, j)),
        out_shape=jax.ShapeDtypeStruct((T, N), F32),
        scratch_shapes=[pltpu.VMEM((tm, D), BF16)],
        compiler_params=_params("parallel", "arbitrary"),
        name="norm_matmul",
    )(h, norm_w.reshape(1, D), w)


def _matmul_residual_kernel(a_ref, w_ref, h_ref, o_ref):
    o_ref[...] = h_ref[...] + _dot(a_ref[...], w_ref[...])


def _matmul_residual(a, w, h):
    T, K = a.shape
    D = w.shape[1]
    tm = _pick_tile(T, 512)
    return pl.pallas_call(
        _matmul_residual_kernel,
        grid=(T // tm,),
        in_specs=[pl.BlockSpec((tm, K), lambda i: (i, 0)),
                  pl.BlockSpec((K, D), lambda i: (0, 0)),
                  pl.BlockSpec((tm, D), lambda i: (i, 0))],
        out_specs=pl.BlockSpec((tm, D), lambda i: (i, 0)),
        out_shape=jax.ShapeDtypeStruct((T, D), F32),
        compiler_params=_params("parallel"),
        name="matmul_residual",
    )(a, w, h)


def _causal_conv_silu(x, tail_ref, w_ref):
    tail = tail_ref[...]
    row = lax.broadcasted_iota(jnp.int32, (8, 1), 0)
    y = x * w_ref[CONV_K - 1:CONV_K, :]
    for j in range(1, CONV_K):
        xr = pltpu.roll(x, j, 0)
        top = jnp.where(row < j, pltpu.roll(tail, j, 0), xr[0:8])
        xs = jnp.concatenate([top, xr[8:]], axis=0)
        y = y + xs * w_ref[CONV_K - 1 - j:CONV_K - j, :]
    tail_ref[...] = x[CHUNK - 8:CHUNK]
    return _silu(y)


def _tril(dtype=F32, k=0):
    r = lax.broadcasted_iota(jnp.int32, (CHUNK, CHUNK), 0)
    c = lax.broadcasted_iota(jnp.int32, (CHUNK, CHUNK), 1)
    return (r - c >= k) if dtype is None else (r - c >= k).astype(dtype)


def _head_layernorm(y):
    mu = jnp.mean(y, axis=-1, keepdims=True)
    yc = y - mu
    return yc * lax.rsqrt(jnp.mean(yc * yc, axis=-1, keepdims=True) + NORM_EPS)


def _ab_mixer_kernel(rq_ref, rk_ref, rv_ref, rg_ref, mq_ref, mk_ref, mv_ref, mo_ref, g_ref,
                     cos_ref, sin_ref, dmat_ref, xi_ref, zeta_ref, gch_ref,
                     convq_ref, convk_ref, gbias_ref, gnr_ref, gnm_ref,
                     o_ref, rstate_ref, mstate_ref, mmax_ref, tailq_ref, tailk_ref):
    c = pl.program_id(1)

    @pl.when(c == 0)
    def _():
        rstate_ref[...] = jnp.zeros_like(rstate_ref)
        mstate_ref[...] = jnp.zeros_like(mstate_ref)
        mmax_ref[...] = jnp.zeros_like(mmax_ref)
        tailq_ref[...] = jnp.zeros_like(tailq_ref)
        tailk_ref[...] = jnp.zeros_like(tailk_ref)

    scale = HEAD_DIM ** -0.5
    cos = cos_ref[...]
    sin = sin_ref[...]

    for h in range(RET_HEADS):
        sl = slice(h * HEAD_DIM, (h + 1) * HEAD_DIM)
        q = rq_ref[:, sl]
        k = rk_ref[:, sl]
        v = rv_ref[:, sl]
        q = q * cos + pltpu.roll(q, HEAD_DIM // 2, 1) * sin
        k = (k * cos + pltpu.roll(k, HEAD_DIM // 2, 1) * sin) * scale
        qb, kb, vb = q.astype(BF16), k.astype(BF16), v.astype(BF16)
        st = rstate_ref[h]
        s = _dot_nt(qb, kb) * dmat_ref[h]
        o = _dot(s.astype(BF16), vb) + _dot(qb, st.astype(BF16)) * xi_ref[:, h:h + 1]
        kz = (k * zeta_ref[:, h:h + 1]).T
        rstate_ref[h] = gch_ref[h] * st + _dot(kz.astype(BF16), vb)
        y = _head_layernorm(o) * gnr_ref[:, sl] * _silu(rg_ref[:, sl])
        o_ref[:, sl] = y.astype(o_ref.dtype)

    mq = _causal_conv_silu(mq_ref[...], tailq_ref, convq_ref)
    mk = _causal_conv_silu(mk_ref[...], tailk_ref, convk_ref) * scale
    gates = g_ref[...] + gbias_ref[...]
    lf = jnp.minimum(gates, 0.0) - jnp.log(1.0 + jnp.exp(-jnp.abs(gates)))
    bcum = _dot(_tril(), lf, precision=HIGHEST)
    gates_t = gates.T
    bcum_t = bcum.T
    causal = _tril(None)
    lane = lax.broadcasted_iota(jnp.int32, (CHUNK, 2 * HEAD_DIM), 1)
    for h in range(ML_HEADS):
        sl = slice(h * HEAD_DIM, (h + 1) * HEAD_DIM)
        q = mq[:, sl]
        k = mk[:, sl]
        v = mv_ref[:, sl]
        i_col = gates[:, h:h + 1]
        i_row = gates_t[h:h + 1, :]
        b_col = bcum[:, ML_HEADS + h:ML_HEADS + h + 1]
        b_row = bcum_t[ML_HEADS + h:ML_HEADS + h + 1, :]
        m_st = mmax_ref[h]
        dlog = jnp.where(causal, b_col - b_row + i_row, NEG_INF)
        inter_log = b_col + m_st
        m_t = jnp.maximum(jnp.max(dlog, axis=-1, keepdims=True), inter_log)
        dw = jnp.exp(dlog - m_t)
        inter_w = jnp.exp(inter_log - m_t)
        qb, kb = q.astype(BF16), k.astype(BF16)
        v_aug = jnp.where(lane < HEAD_DIM, jnp.concatenate([v, v], axis=1),
                          (lane == HEAD_DIM).astype(F32)).astype(BF16)
        st = mstate_ref[h]
        s = _dot_nt(qb, kb) * dw
        nd = _dot(s.astype(BF16), v_aug) + inter_w * _dot(qb, st.astype(BF16))
        num = nd[:, :HEAD_DIM]
        den = nd[:, HEAD_DIM:HEAD_DIM + 1]
        hh = num / jnp.maximum(jnp.abs(den), jnp.exp(-m_t))
        b_last = b_col[CHUNK - 1:CHUNK, :]
        w_log = b_last - b_col + i_col
        m_new = jnp.maximum(b_last + m_st, jnp.max(w_log, axis=0, keepdims=True))
        w = jnp.exp(w_log - m_new)
        dec = jnp.exp(b_last + m_st - m_new)
        kw = (k * w).T
        mstate_ref[h] = dec * st + _dot(kw.astype(BF16), v_aug)
        mmax_ref[h] = m_new
        osl = slice(RET_HEADS * HEAD_DIM + h * HEAD_DIM, RET_HEADS * HEAD_DIM + (h + 1) * HEAD_DIM)
        y = _head_layernorm(hh) * gnm_ref[:, sl] * _sigmoid(mo_ref[:, sl])
        o_ref[:, osl] = y.astype(o_ref.dtype)


def _ab_mixer(z, B, S, conv_w, b_i, b_f, gn_ret, gn_mlstm):
    N = z.shape[1]
    z3 = z.reshape(B, S, N)
    nc = S // CHUNK
    rw = RET_HEADS * HEAD_DIM
    mw = ML_HEADS * HEAD_DIM

    pos = jnp.arange(S, dtype=F32)
    inv = 1.0 / (ROPE_BASE ** (jnp.arange(0, HEAD_DIM, 2, dtype=F32) / HEAD_DIM))
    ang = pos[:, None] * inv[None, :]
    cos = jnp.concatenate([jnp.cos(ang), jnp.cos(ang)], axis=1)
    sin = jnp.concatenate([-jnp.sin(ang), jnp.sin(ang)], axis=1)
    log_g = jnp.log(1.0 - 2.0 ** (-5.0 - jnp.arange(RET_HEADS, dtype=F32)))
    idx = jnp.arange(CHUNK, dtype=F32)
    diff = idx[:, None] - idx[None, :]
    cm = diff >= 0
    dmat = jnp.where(cm[None], jnp.exp(jnp.where(cm, diff, 0.0)[None] * log_g[:, None, None]), 0.0)
    xi = jnp.exp((idx + 1.0)[:, None] * log_g[None, :])
    zeta = jnp.exp((CHUNK - 1.0 - idx)[:, None] * log_g[None, :])
    gch = jnp.exp(CHUNK * log_g).reshape(RET_HEADS, 1, 1)
    gbias = jnp.zeros((1, LANES), F32).at[0, :ML_HEADS].set(b_i).at[0, ML_HEADS:2 * ML_HEADS].set(b_f)

    def zcol(width, blk):
        return pl.BlockSpec((None, CHUNK, width), lambda b, c: (b, c, blk))

    def full(shape):
        return pl.BlockSpec(shape, lambda b, c: (0,) * len(shape))

    return pl.pallas_call(
        _ab_mixer_kernel,
        grid=(B, nc),
        in_specs=[zcol(rw, 0), zcol(rw, 1), zcol(rw, 2), zcol(rw, 3),
                  zcol(mw, 4), zcol(mw, 5), zcol(mw, 6), zcol(mw, 7),
                  zcol(LANES, (4 * rw + 4 * mw) // LANES),
                  pl.BlockSpec((CHUNK, HEAD_DIM), lambda b, c: (c, 0)),
                  pl.BlockSpec((CHUNK, HEAD_DIM), lambda b, c: (c, 0)),
                  full((RET_HEADS, CHUNK, CHUNK)), full((CHUNK, RET_HEADS)), full((CHUNK, RET_HEADS)),
                  full((RET_HEADS, 1, 1)),
                  full((CONV_K, mw)), full((CONV_K, mw)), full((1, LANES)),
                  full((1, rw)), full((1, mw))],
        out_specs=pl.BlockSpec((None, CHUNK, rw + mw), lambda b, c: (b, c, 0)),
        out_shape=jax.ShapeDtypeStruct((B, S, rw + mw), BF16),
        scratch_shapes=[pltpu.VMEM((RET_HEADS, HEAD_DIM, HEAD_DIM), F32),
                        pltpu.VMEM((ML_HEADS, HEAD_DIM, 2 * HEAD_DIM), F32),
                        pltpu.VMEM((ML_HEADS, 1, 1), F32),
                        pltpu.VMEM((8, mw), F32),
                        pltpu.VMEM((8, mw), F32)],
        compiler_params=_params("parallel", "arbitrary"),
        name="ab_mixer",
    )(z3, z3, z3, z3, z3, z3, z3, z3, z3, cos, sin, dmat, xi, zeta, gch,
      conv_w[:, :mw], conv_w[:, mw:], gbias, gn_ret.reshape(1, rw), gn_mlstm.reshape(1, mw)
      ).reshape(B * S, rw + mw)


def _split(x):
    hi = x.astype(BF16)
    return hi, (x - hi.astype(F32)).astype(BF16)


def _split_dot(a, b):
    return _dot(a[0], b[0]) + (_dot(a[0], b[1]) + _dot(a[1], b[0]))


def _unit_lower_inverses(mats):
    r = lax.broadcasted_iota(jnp.int32, (CHUNK, CHUNK), 0)
    c = lax.broadcasted_iota(jnp.int32, (CHUNK, CHUNK), 1)
    eye = (r == c).astype(F32)
    ps = [-a for a in mats]
    ts = [eye + p for p in ps]
    n = 2
    parts = [_split(p) for p in ps]
    while n < CHUNK:
        parts = [_split(_split_dot(p, p)) for p in parts]
        ts = [t + _split_dot(_split(t), p) for t, p in zip(ts, parts)]
        n *= 2
    return ts


def _dn_mixer_kernel(qkv_ref, gate_ref, g_ref, conv_ref, gpar_ref, nw_ref,
                     o_ref, state_ref, tail_ref):
    c = pl.program_id(1)

    @pl.when(c == 0)
    def _():
        state_ref[...] = jnp.zeros_like(state_ref)
        tail_ref[...] = jnp.zeros_like(tail_ref)

    w = DN_HEADS * HEAD_DIM
    qkv = _causal_conv_silu(qkv_ref[...], tail_ref, conv_ref)
    gates = g_ref[...]
    beta_all = _sigmoid(gates)
    g_all = -jnp.exp(gpar_ref[1:2, :]) * _softplus(gates + gpar_ref[0:1, :])
    gcum = _dot(_tril(), g_all, precision=HIGHEST)
    gcum_t = gcum.T
    lower = _tril(None)
    strict = _tril(None, 1)
    heads = range(DN_HEADS)

    a_mats, locs = [], []
    for h in heads:
        q = qkv[:, h * HEAD_DIM:(h + 1) * HEAD_DIM]
        k = qkv[:, w + h * HEAD_DIM:w + (h + 1) * HEAD_DIM]
        v = qkv[:, 2 * w + h * HEAD_DIM:2 * w + (h + 1) * HEAD_DIM]
        q = q * lax.rsqrt(jnp.sum(q * q, axis=-1, keepdims=True) + NORM_EPS) * HEAD_DIM ** -0.5
        k = k * lax.rsqrt(jnp.sum(k * k, axis=-1, keepdims=True) + NORM_EPS)
        beta = beta_all[:, h:h + 1]
        g_col = gcum[:, DN_HEADS + h:DN_HEADS + h + 1]
        g_row = gcum_t[DN_HEADS + h:DN_HEADS + h + 1, :]
        g_last = g_col[CHUNK - 1:CHUNK, :]
        decay = jnp.exp(jnp.where(lower, g_col - g_row, NEG_INF))
        kb = k * beta
        kbf = k.astype(BF16)
        a_mats.append(jnp.where(strict, _dot_nt(kb.astype(BF16), kbf) * decay, 0.0))
        eg = jnp.exp(g_col)
        locs.append(dict(
            rhs=jnp.concatenate([v * beta, kb * eg], axis=1),
            attn=(_dot_nt(q.astype(BF16), kbf) * decay).astype(BF16),
            q_dec=(q * eg).astype(BF16),
            k_dec_t=(k * jnp.exp(g_last - g_col)).T.astype(BF16),
            g_last=jnp.exp(g_last)))

    t_invs = _unit_lower_inverses(a_mats)
    uws = [_split_dot(_split(t), _split(loc["rhs"])) for t, loc in zip(t_invs, locs)]

    for h, uw, loc in zip(heads, uws, locs):
        sl = slice(h * HEAD_DIM, (h + 1) * HEAD_DIM)
        u = uw[:, :HEAD_DIM]
        wm = uw[:, HEAD_DIM:]
        st = state_ref[h]
        stb = st.astype(BF16)
        vnb = (u - _dot(wm.astype(BF16), stb)).astype(BF16)
        o = _dot(loc["q_dec"], stb) + _dot(loc["attn"], vnb)
        state_ref[h] = loc["g_last"] * st + _dot(loc["k_dec_t"], vnb)
        y = _rms(o, nw_ref[...]) * _silu(gate_ref[:, sl])
        o_ref[:, sl] = y.astype(o_ref.dtype)


def _dn_mixer(z, B, S, conv_w, a_log, dt_bias, norm_w):
    N = z.shape[1]
    z3 = z.reshape(B, S, N)
    nc = S // CHUNK
    w = DN_HEADS * HEAD_DIM
    gpar = jnp.zeros((8, LANES), F32)
    gpar = gpar.at[0, DN_HEADS:2 * DN_HEADS].set(dt_bias).at[1, DN_HEADS:2 * DN_HEADS].set(a_log)

    def full(shape):
        return pl.BlockSpec(shape, lambda b, c: (0,) * len(shape))

    return pl.pallas_call(
        _dn_mixer_kernel,
        grid=(B, nc),
        in_specs=[pl.BlockSpec((None, CHUNK, 3 * w), lambda b, c: (b, c, 0)),
                  pl.BlockSpec((None, CHUNK, w), lambda b, c: (b, c, 3)),
                  pl.BlockSpec((None, CHUNK, LANES), lambda b, c: (b, c, 4 * w // LANES)),
                  full((CONV_K, 3 * w)), full((8, LANES)), full((1, HEAD_DIM))],
        out_specs=pl.BlockSpec((None, CHUNK, w), lambda b, c: (b, c, 0)),
        out_shape=jax.ShapeDtypeStruct((B, S, w), BF16),
        scratch_shapes=[pltpu.VMEM((DN_HEADS, HEAD_DIM, HEAD_DIM), F32),
                        pltpu.VMEM((8, 3 * w), F32)],
        compiler_params=_params("parallel", "arbitrary"),
        name="dn_mixer",
    )(z3, z3, z3, conv_w, gpar, norm_w.reshape(1, HEAD_DIM)).reshape(B * S, w)


def _pack_rows(x):
    return pltpu.bitcast(x.astype(BF16), jnp.uint32)


def _candidate_pairs(n):
    return [(i, j) for i in range(n) for j in range(n) if (i + 1) * (j + 1) <= n]


def _peer_route_kernel(h_ref, nw_ref, wq_ref, keys_ref, xn_ref, cnt_ref, e1_ref, qe_ref,
                       s_ref, work_ref, top_ref, st_ref, rank_ref):
    tt = xn_ref.shape[1]
    xn = _rms(h_ref[...], nw_ref[...])
    xn_t = xn.T.astype(BF16)
    xn_ref[...] = xn_t
    q_t = _dot(wq_ref[...], xn_t)
    for h in range(PEER_HEADS):
        for p in range(2):
            lo = (2 * h + p) * PEER_NK
            s = _dot(keys_ref[p], q_t[lo:lo + PEER_NK].astype(BF16))
            s_ref[p, h] = s
            work_ref[p, h] = s

    n_top = PEER_TOPK + 1

    rank_ref[...] = jnp.full(rank_ref.shape, float(PEER_TOPK), F32)

    def extract(r, carry):
        for p in range(2):
            for h in range(PEER_HEADS):
                cur = work_ref[p, h]
                m = jnp.max(cur, axis=0, keepdims=True)
                top_ref[p, pl.ds(r, 1), h:h + 1, :] = m[None]
                hit = cur >= m
                work_ref[p, h] = jnp.where(hit, NEG_INF, cur)
                if p == 1:
                    rank_ref[h] = jnp.where(hit, jnp.minimum(r, PEER_TOPK).astype(F32), rank_ref[h])
        return carry

    lax.fori_loop(0, n_top, extract, 0)

    a = [top_ref[0, r] for r in range(n_top)]
    b = [top_ref[1, r] for r in range(n_top)]
    cands = [a[i] + b[j] for i, j in _candidate_pairs(n_top)]
    cur = list(cands)
    for _ in range(PEER_TOPK - 1):
        m = functools.reduce(jnp.maximum, cur)
        cur = [jnp.where(x >= m, NEG_INF, x) for x in cur]
    kth = functools.reduce(jnp.maximum, cur)
    cur = [jnp.where(x >= kth, NEG_INF, x) for x in cur]
    nxt = functools.reduce(jnp.maximum, cur)
    theta = jnp.where(nxt > NEG_INF, 0.5 * (kth + nxt), kth)
    top = a[0] + b[0]
    z = functools.reduce(lambda u, v: u + v,
                         [jnp.where(x >= theta, jnp.exp(x - top), 0.0) for x in cands])
    st_ref[0:8, :] = theta
    st_ref[8:16, :] = a[0]
    st_ref[16:24, :] = b[0]
    st_ref[24:32, :] = 0.5 / z

    groups = PEER_NK // BF16_ROWS
    for h in range(PEER_HEADS):
        s1 = s_ref[0, h]
        s2 = s_ref[1, h]
        tau = st_ref[h:h + 1, :] - s1
        cnt = jnp.zeros_like(s1)
        for r in range(PEER_TOPK):
            cnt = jnp.where(top_ref[1, r, h:h + 1, :] >= tau, float(r + 1), cnt)
        cnt_ref[h] = cnt
        rank = rank_ref[h]
        e1_ref[h] = jnp.exp(s1 - st_ref[8 + h:9 + h, :]) * st_ref[24 + h:25 + h, :]
        e2 = jnp.exp(s2 - st_ref[16 + h:17 + h, :])
        for c in range(tt // LANES):
            lanes = slice(c * LANES, (c + 1) * LANES)
            qe_ref[h, c, :, 0] = _pack_rows(rank[:, lanes]).reshape(groups, SUBLANES, LANES)
            qe_ref[h, c, :, 1] = _pack_rows(e2[:, lanes]).reshape(groups, SUBLANES, LANES)


def _peer_route(h, norm_w, wq_t, keys, tt):
    T, D = h.shape
    nt = T // tt
    nq = wq_t.shape[0]
    hs = (PEER_HEADS, PEER_NK, tt)
    qs = (PEER_HEADS, tt // LANES, PEER_NK // BF16_ROWS, 2, SUBLANES, LANES)
    return pl.pallas_call(
        _peer_route_kernel,
        grid=(nt,),
        in_specs=[pl.BlockSpec((tt, D), lambda i: (i, 0)),
                  pl.BlockSpec((1, D), lambda i: (0, 0)),
                  pl.BlockSpec((nq, D), lambda i: (0, 0)),
                  pl.BlockSpec((2, PEER_NK, PEER_NK), lambda i: (0, 0, 0))],
        out_specs=[pl.BlockSpec((None, D, tt), lambda i: (i, 0, 0)),
                   pl.BlockSpec((None,) + hs, lambda i: (i, 0, 0, 0)),
                   pl.BlockSpec((None,) + hs, lambda i: (i, 0, 0, 0)),
                   pl.BlockSpec((None,) + qs, lambda i: (i, 0, 0, 0, 0, 0, 0))],
        out_shape=[jax.ShapeDtypeStruct((nt, D, tt), BF16),
                   jax.ShapeDtypeStruct((nt,) + hs, F32),
                   jax.ShapeDtypeStruct((nt,) + hs, F32),
                   jax.ShapeDtypeStruct((nt,) + qs, jnp.uint32)],
        scratch_shapes=[pltpu.VMEM((2,) + hs, F32),
                        pltpu.VMEM((2,) + hs, F32),
                        pltpu.VMEM((2, PEER_TOPK + 1, PEER_HEADS, tt), F32),
                        pltpu.VMEM((4 * PEER_HEADS, tt), F32),
                        pltpu.VMEM(hs, F32)],
        compiler_params=_params("parallel"),
        name="peer_route",
    )(h, norm_w.reshape(1, D), wq_t, keys)


def _peer_dense_kernel(xn_ref, u_ref, vt_ref, cnt_ref, e1_ref, qe_ref, h_ref, o_ref,
                       acc_ref, act_ref, wg_ref):
    j = pl.program_id(1)
    eb, tt = wg_ref.shape
    nblk = eb // PEER_NK

    @pl.when(j == 0)
    def _():
        acc_ref[...] = jnp.zeros_like(acc_ref)

    act_ref[...] = _dot(u_ref[...], xn_ref[...])
    for kb in range(nblk):
        i1 = j * nblk + kb
        cnt_rows = [cnt_ref[h, pl.ds(i1, 1), :] for h in range(PEER_HEADS)]
        e1_rows = [e1_ref[h, pl.ds(i1, 1), :] for h in range(PEER_HEADS)]
        for c in range(tt // LANES):
            lanes = slice(c * LANES, (c + 1) * LANES)
            shape = (BF16_ROWS, LANES)
            cnts = [jnp.broadcast_to(r[:, lanes], shape).astype(BF16) for r in cnt_rows]
            e1s = [jnp.broadcast_to(r[:, lanes], shape).astype(BF16) for r in e1_rows]
            for g in range(PEER_NK // BF16_ROWS):
                w = None
                for h in range(PEER_HEADS):
                    rank = pltpu.bitcast(qe_ref[h, c, g, 0], BF16)
                    e2 = pltpu.bitcast(qe_ref[h, c, g, 1], BF16)
                    t = jnp.where(rank < cnts[h], e2, 0.0) * e1s[h]
                    w = t if w is None else w + t
                rows = slice(kb * PEER_NK + g * BF16_ROWS, kb * PEER_NK + (g + 1) * BF16_ROWS)
                a = act_ref[rows, lanes]
                gated = (w.astype(F32) * a) * (1.0 + lax.erf(a * (1.0 / math.sqrt(2.0))))
                wg_ref[rows, lanes] = gated.astype(BF16)
    acc_ref[...] += _dot(vt_ref[...], wg_ref[...])

    @pl.when(j == pl.num_programs(1) - 1)
    def _():
        o_ref[...] = h_ref[...] + acc_ref[...].T


def _peer_dense(xn_t, u, v_blocks, cnt, e1, qe, h):
    nt, D, tt = xn_t.shape
    nb, _, eb = v_blocks.shape
    hs = (PEER_HEADS, PEER_NK, tt)
    qs = qe.shape[1:]
    return pl.pallas_call(
        _peer_dense_kernel,
        grid=(nt, nb),
        in_specs=[pl.BlockSpec((None, D, tt), lambda i, j: (i, 0, 0)),
                  pl.BlockSpec((eb, D), lambda i, j: (j, 0)),
                  pl.BlockSpec((None, D, eb), lambda i, j: (j, 0, 0)),
                  pl.BlockSpec((None,) + hs, lambda i, j: (i, 0, 0, 0)),
                  pl.BlockSpec((None,) + hs, lambda i, j: (i, 0, 0, 0)),
                  pl.BlockSpec((None,) + qs, lambda i, j: (i, 0, 0, 0, 0, 0, 0)),
                  pl.BlockSpec((tt, D), lambda i, j: (i, 0))],
        out_specs=pl.BlockSpec((tt, D), lambda i, j: (i, 0)),
        out_shape=jax.ShapeDtypeStruct(h.shape, F32),
        scratch_shapes=[pltpu.VMEM((D, tt), F32),
                        pltpu.VMEM((eb, tt), F32),
                        pltpu.VMEM((eb, tt), BF16)],
        compiler_params=_params("parallel", "arbitrary"),
        name="peer_dense",
    )(xn_t, u, v_blocks, cnt, e1, qe, h)


def _ple_kernel(h_ref, p_ref, nw_ref, wg_ref, wp_ref, fw_ref, o_ref, *, final_norm):
    h = h_ref[...]
    gate = _sigmoid(_dot(_rms(h, nw_ref[...]).astype(BF16), wg_ref[...]))
    e = _dot(p_ref[...].astype(BF16), wp_ref[...])
    out = h + gate * e
    if final_norm:
        out = _rms(out, fw_ref[...])
    o_ref[...] = out


def _ple(h, p, norm_w, w_gate, w_proj, final_w, final_norm):
    T, D = h.shape
    P = p.shape[1]
    tm = _pick_tile(T, 512)
    return pl.pallas_call(
        functools.partial(_ple_kernel, final_norm=final_norm),
        grid=(T // tm,),
        in_specs=[pl.BlockSpec((tm, D), lambda i: (i, 0)),
                  pl.BlockSpec((tm, P), lambda i: (i, 0)),
                  pl.BlockSpec((1, D), lambda i: (0, 0)),
                  pl.BlockSpec((D, D), lambda i: (0, 0)),
                  pl.BlockSpec((P, D), lambda i: (0, 0)),
                  pl.BlockSpec((1, D), lambda i: (0, 0))],
        out_specs=pl.BlockSpec((tm, D), lambda i: (i, 0)),
        out_shape=jax.ShapeDtypeStruct((T, D), F32),
        compiler_params=_params("parallel"),
        name="ple",
    )(h, p, norm_w.reshape(1, D), w_gate, w_proj, final_w.reshape(1, D))


def _pad_cols(w, mult=LANES):
    n = w.shape[1]
    return jnp.pad(w, ((0, 0), (0, (-n) % mult)))


def kernel(x, p, norm_mix_w, norm_ffn_w, ab_w_in, ab_conv_w, ab_b_i, ab_b_f, ab_gn_ret, ab_gn_mlstm, ab_w_out, dn_w_in, dn_conv_w, dn_a_log, dn_dt_bias, dn_norm_w, dn_w_out, peer_w_q, peer_sub_keys, peer_u, peer_v, ple_w_proj, ple_w_gate, ple_norm_w, final_norm_w):
    B, S, D = x.shape
    depth = p.shape[0]
    T = B * S
    h = x.reshape(T, D)
    for i in range(depth):
        j = i // 2
        if i % 2 == 0:
            z = _norm_matmul(h, norm_mix_w[i], _pad_cols(ab_w_in[j]).astype(BF16))
            mixed = _ab_mixer(z, B, S, ab_conv_w[j], ab_b_i[j], ab_b_f[j], ab_gn_ret[j], ab_gn_mlstm[j])
            h = _matmul_residual(mixed, ab_w_out[j].astype(BF16), h)
        else:
            z = _norm_matmul(h, norm_mix_w[i], _pad_cols(dn_w_in[j]).astype(BF16))
            mixed = _dn_mixer(z, B, S, dn_conv_w[j], dn_a_log[j], dn_dt_bias[j], dn_norm_w[j])
            h = _matmul_residual(mixed, dn_w_out[j].astype(BF16), h)
        xn_t, cnt, e1, qe = _peer_route(h, norm_ffn_w[i], peer_w_q[i].T.astype(BF16),
                                        peer_sub_keys[i].astype(BF16), _pick_tile(T, PEER_TOKEN_TILE))
        n_exp = peer_v.shape[1]
        v_blocks = peer_v[i].astype(BF16).reshape(n_exp // PEER_EXPERT_BLOCK, PEER_EXPERT_BLOCK, D)
        h = _peer_dense(xn_t, peer_u[i].astype(BF16), v_blocks.transpose(0, 2, 1), cnt, e1, qe, h)
        h = _ple(h, p[i].reshape(T, -1), ple_norm_w[i], ple_w_gate[i].astype(BF16),
                 ple_w_proj[i].astype(BF16), final_norm_w, final_norm=(i == depth - 1))
    return h.reshape(B, S, D)
```

```python
import functools
import math

import jax
import jax.numpy as jnp
from jax import lax
from jax.experimental import pallas as pl
from jax.experimental.pallas import tpu as pltpu

F32 = jnp.float32
BF16 = jnp.bfloat16

LANES = 128
SUBLANES = 8
BF16_ROWS = 16
CHUNK = 128
HEAD_DIM = 128
CONV_K = 4
NORM_EPS = 1e-6
ROPE_BASE = 10000.0
RET_HEADS = 4
ML_HEADS = 4
DN_HEADS = 8
PEER_HEADS = 8
PEER_NK = 128
PEER_TOPK = 16
PEER_TOKEN_TILE = 512
PEER_EXPERT_BLOCK = 1024
VMEM_LIMIT = 48 * 1024 * 1024

HIGHEST = lax.Precision.HIGHEST
NEG_INF = float("-inf")


def _dot(a, b, precision=None):
    return jnp.dot(a, b, preferred_element_type=F32, precision=precision)


def _dot_nt(a, b, precision=None):
    return lax.dot_general(a, b, (((1,), (1,)), ((), ())), preferred_element_type=F32,
                           precision=precision)


def _rms(x, w):
    return x * lax.rsqrt(jnp.mean(x * x, axis=-1, keepdims=True) + NORM_EPS) * w


def _sigmoid(x):
    return 1.0 / (1.0 + jnp.exp(-x))


def _silu(x):
    return x * _sigmoid(x)


def _softplus(x):
    return jnp.maximum(x, 0.0) + jnp.log(1.0 + jnp.exp(-jnp.abs(x)))


def _params(*sem, flags=None):
    return pltpu.CompilerParams(dimension_semantics=sem, vmem_limit_bytes=VMEM_LIMIT, flags=flags)


def _pick_tile(n, cap):
    t = min(n, cap)
    while n % t:
        t //= 2
    return t


def _norm_matmul_kernel(x_ref, nw_ref, w_ref, o_ref, xn_ref):
    @pl.when(pl.program_id(1) == 0)
    def _():
        xn_ref[...] = _rms(x_ref[...], nw_ref[...]).astype(BF16)

    o_ref[...] = _dot(xn_ref[...], w_ref[...])


def _norm_matmul(h, norm_w, w):
    T, D = h.shape
    N = w.shape[1]
    tm = _pick_tile(T, 1024)
    tn = N
    for cand in range(N // LANES, 0, -1):
        if (N // LANES) % cand == 0 and cand * LANES <= 1536:
            tn = cand * LANES
            break
    return pl.pallas_call(
        _norm_matmul_kernel,
        grid=(T // tm, N // tn),
        in_specs=[pl.BlockSpec((tm, D), lambda i, j: (i, 0)),
                  pl.BlockSpec((1, D), lambda i, j: (0, 0)),
                  pl.BlockSpec((D, tn), lambda i, j: (0, j))],
        out_specs=pl.BlockSpec((tm, tn), lambda i, j: (i, j)),
        out_shape=jax.ShapeDtypeStruct((T, N), F32),
        scratch_shapes=[pltpu.VMEM((tm, D), BF16)],
        compiler_params=_params("parallel", "arbitrary"),
        name="norm_matmul",
    )(h, norm_w.reshape(1, D), w)


def _matmul_residual_kernel(a_ref, w_ref, h_ref, o_ref):
    o_ref[...] = h_ref[...] + _dot(a_ref[...], w_ref[...])


def _matmul_residual(a, w, h):
    T, K = a.shape
    D = w.shape[1]
    tm = _pick_tile(T, 512)
    return pl.pallas_call(
        _matmul_residual_kernel,
        grid=(T // tm,),
        in_specs=[pl.BlockSpec((tm, K), lambda i: (i, 0)),
                  pl.BlockSpec((K, D), lambda i: (0, 0)),
                  pl.BlockSpec((tm, D), lambda i: (i, 0))],
        out_specs=pl.BlockSpec((tm, D), lambda i: (i, 0)),
        out_shape=jax.ShapeDtypeStruct((T, D), F32),
        compiler_params=_params("parallel"),
        name="matmul_residual",
    )(a, w, h)


def _causal_conv_silu(x, tail_ref, w_ref):
    tail = tail_ref[...]
    row = lax.broadcasted_iota(jnp.int32, (8, 1), 0)
    y = x * w_ref[CONV_K - 1:CONV_K, :]
    for j in range(1, CONV_K):
        xr = pltpu.roll(x, j, 0)
        top = jnp.where(row < j, pltpu.roll(tail, j, 0), xr[0:8])
        xs = jnp.concatenate([top, xr[8:]], axis=0)
        y = y + xs * w_ref[CONV_K - 1 - j:CONV_K - j, :]
    tail_ref[...] = x[CHUNK - 8:CHUNK]
    return _silu(y)


def _tril(dtype=F32, k=0):
    r = lax.broadcasted_iota(jnp.int32, (CHUNK, CHUNK), 0)
    c = lax.broadcasted_iota(jnp.int32, (CHUNK, CHUNK), 1)
    return (r - c >= k) if dtype is None else (r - c >= k).astype(dtype)


def _head_layernorm(y):
    mu = jnp.mean(y, axis=-1, keepdims=True)
    yc = y - mu
    return yc * lax.rsqrt(jnp.mean(yc * yc, axis=-1, keepdims=True) + NORM_EPS)


def _ab_mixer_kernel(rq_ref, rk_ref, rv_ref, rg_ref, mq_ref, mk_ref, mv_ref, mo_ref, g_ref,
                     cos_ref, sin_ref, dmat_ref, xi_ref, zeta_ref, gch_ref,
                     convq_ref, convk_ref, gbias_ref, gnr_ref, gnm_ref,
                     o_ref, rstate_ref, mstate_ref, mmax_ref, tailq_ref, tailk_ref):
    c = pl.program_id(1)

    @pl.when(c == 0)
    def _():
        rstate_ref[...] = jnp.zeros_like(rstate_ref)
        mstate_ref[...] = jnp.zeros_like(mstate_ref)
        mmax_ref[...] = jnp.zeros_like(mmax_ref)
        tailq_ref[...] = jnp.zeros_like(tailq_ref)
        tailk_ref[...] = jnp.zeros_like(tailk_ref)

    scale = HEAD_DIM ** -0.5
    cos = cos_ref[...]
    sin = sin_ref[...]

    for h in range(RET_HEADS):
        sl = slice(h * HEAD_DIM, (h + 1) * HEAD_DIM)
        q = rq_ref[:, sl]
        k = rk_ref[:, sl]
        v = rv_ref[:, sl]
        q = q * cos + pltpu.roll(q, HEAD_DIM // 2, 1) * sin
        k = (k * cos + pltpu.roll(k, HEAD_DIM // 2, 1) * sin) * scale
        qb, kb, vb = q.astype(BF16), k.astype(BF16), v.astype(BF16)
        st = rstate_ref[h]
        s = _dot_nt(qb, kb) * dmat_ref[h]
        o = _dot(s.astype(BF16), vb) + _dot(qb, st.astype(BF16)) * xi_ref[:, h:h + 1]
        kz = (k * zeta_ref[:, h:h + 1]).T
        rstate_ref[h] = gch_ref[h] * st + _dot(kz.astype(BF16), vb)
        y = _head_layernorm(o) * gnr_ref[:, sl] * _silu(rg_ref[:, sl])
        o_ref[:, sl] = y.astype(o_ref.dtype)

    mq = _causal_conv_silu(mq_ref[...], tailq_ref, convq_ref)
    mk = _causal_conv_silu(mk_ref[...], tailk_ref, convk_ref) * scale
    gates = g_ref[...] + gbias_ref[...]
    lf = jnp.minimum(gates, 0.0) - jnp.log(1.0 + jnp.exp(-jnp.abs(gates)))
    bcum = _dot(_tril(), lf, precision=HIGHEST)
    gates_t = gates.T
    bcum_t = bcum.T
    causal = _tril(None)
    lane = lax.broadcasted_iota(jnp.int32, (CHUNK, 2 * HEAD_DIM), 1)
    for h in range(ML_HEADS):
        sl = slice(h * HEAD_DIM, (h + 1) * HEAD_DIM)
        q = mq[:, sl]
        k = mk[:, sl]
        v = mv_ref[:, sl]
        i_col = gates[:, h:h + 1]
        i_row = gates_t[h:h + 1, :]
        b_col = bcum[:, ML_HEADS + h:ML_HEADS + h + 1]
        b_row = bcum_t[ML_HEADS + h:ML_HEADS + h + 1, :]
        m_st = mmax_ref[h]
        dlog = jnp.where(causal, b_col - b_row + i_row, NEG_INF)
        inter_log = b_col + m_st
        m_t = jnp.maximum(jnp.max(dlog, axis=-1, keepdims=True), inter_log)
        dw = jnp.exp(dlog - m_t)
        inter_w = jnp.exp(inter_log - m_t)
        qb, kb = q.astype(BF16), k.astype(BF16)
        v_aug = jnp.where(lane < HEAD_DIM, jnp.concatenate([v, v], axis=1),
                          (lane == HEAD_DIM).astype(F32)).astype(BF16)
        st = mstate_ref[h]
        s = _dot_nt(qb, kb) * dw
        nd = _dot(s.astype(BF16), v_aug) + inter_w * _dot(qb, st.astype(BF16))
        num = nd[:, :HEAD_DIM]
        den = nd[:, HEAD_DIM:HEAD_DIM + 1]
        hh = num / jnp.maximum(jnp.abs(den), jnp.exp(-m_t))
        b_last = b_col[CHUNK - 1:CHUNK, :]
        w_log = b_last - b_col + i_col
        m_new = jnp.maximum(b_last + m_st, jnp.max(w_log, axis=0, keepdims=True))
        w = jnp.exp(w_log - m_new)
        dec = jnp.exp(b_last + m_st - m_new)
        kw = (k * w).T
        mstate_ref[h] = dec * st + _dot(kw.astype(BF16), v_aug)
        mmax_ref[h] = m_new
        osl = slice(RET_HEADS * HEAD_DIM + h * HEAD_DIM, RET_HEADS * HEAD_DIM + (h + 1) * HEAD_DIM)
        y = _head_layernorm(hh) * gnm_ref[:, sl] * _sigmoid(mo_ref[:, sl])
        o_ref[:, osl] = y.astype(o_ref.dtype)


def _ab_mixer(z, B, S, conv_w, b_i, b_f, gn_ret, gn_mlstm):
    N = z.shape[1]
    z3 = z.reshape(B, S, N)
    nc = S // CHUNK
    rw = RET_HEADS * HEAD_DIM
    mw = ML_HEADS * HEAD_DIM

    pos = jnp.arange(S, dtype=F32)
    inv = 1.0 / (ROPE_BASE ** (jnp.arange(0, HEAD_DIM, 2, dtype=F32) / HEAD_DIM))
    ang = pos[:, None] * inv[None, :]
    cos = jnp.concatenate([jnp.cos(ang), jnp.cos(ang)], axis=1)
    sin = jnp.concatenate([-jnp.sin(ang), jnp.sin(ang)], axis=1)
    log_g = jnp.log(1.0 - 2.0 ** (-5.0 - jnp.arange(RET_HEADS, dtype=F32)))
    idx = jnp.arange(CHUNK, dtype=F32)
    diff = idx[:, None] - idx[None, :]
    cm = diff >= 0
    dmat = jnp.where(cm[None], jnp.exp(jnp.where(cm, diff, 0.0)[None] * log_g[:, None, None]), 0.0)
    xi = jnp.exp((idx + 1.0)[:, None] * log_g[None, :])
    zeta = jnp.exp((CHUNK - 1.0 - idx)[:, None] * log_g[None, :])
    gch = jnp.exp(CHUNK * log_g).reshape(RET_HEADS, 1, 1)
    gbias = jnp.zeros((1, LANES), F32).at[0, :ML_HEADS].set(b_i).at[0, ML_HEADS:2 * ML_HEADS].set(b_f)

    def zcol(width, blk):
        return pl.BlockSpec((None, CHUNK, width), lambda b, c: (b, c, blk))

    def full(shape):
        return pl.BlockSpec(shape, lambda b, c: (0,) * len(shape))

    return pl.pallas_call(
        _ab_mixer_kernel,
        grid=(B, nc),
        in_specs=[zcol(rw, 0), zcol(rw, 1), zcol(rw, 2), zcol(rw, 3),
                  zcol(mw, 4), zcol(mw, 5), zcol(mw, 6), zcol(mw, 7),
                  zcol(LANES, (4 * rw + 4 * mw) // LANES),
                  pl.BlockSpec((CHUNK, HEAD_DIM), lambda b, c: (c, 0)),
                  pl.BlockSpec((CHUNK, HEAD_DIM), lambda b, c: (c, 0)),
                  full((RET_HEADS, CHUNK, CHUNK)), full((CHUNK, RET_HEADS)), full((CHUNK, RET_HEADS)),
                  full((RET_HEADS, 1, 1)),
                  full((CONV_K, mw)), full((CONV_K, mw)), full((1, LANES)),
                  full((1, rw)), full((1, mw))],
        out_specs=pl.BlockSpec((None, CHUNK, rw + mw), lambda b, c: (b, c, 0)),
        out_shape=jax.ShapeDtypeStruct((B, S, rw + mw), BF16),
        scratch_shapes=[pltpu.VMEM((RET_HEADS, HEAD_DIM, HEAD_DIM), F32),
                        pltpu.VMEM((ML_HEADS, HEAD_DIM, 2 * HEAD_DIM), F32),
                        pltpu.VMEM((ML_HEADS, 1, 1), F32),
                        pltpu.VMEM((8, mw), F32),
                        pltpu.VMEM((8, mw), F32)],
        compiler_params=_params("parallel", "arbitrary"),
        name="ab_mixer",
    )(z3, z3, z3, z3, z3, z3, z3, z3, z3, cos, sin, dmat, xi, zeta, gch,
      conv_w[:, :mw], conv_w[:, mw:], gbias, gn_ret.reshape(1, rw), gn_mlstm.reshape(1, mw)
      ).reshape(B * S, rw + mw)


def _split(x):
    hi = x.astype(BF16)
    return hi, (x - hi.astype(F32)).astype(BF16)


def _split_dot(a, b):
    return _dot(a[0], b[0]) + (_dot(a[0], b[1]) + _dot(a[1], b[0]))


def _unit_lower_inverses(mats):
    r = lax.broadcasted_iota(jnp.int32, (CHUNK, CHUNK), 0)
    c = lax.broadcasted_iota(jnp.int32, (CHUNK, CHUNK), 1)
    eye = (r == c).astype(F32)
    ps = [-a for a in mats]
    ts = [eye + p for p in ps]
    n = 2
    parts = [_split(p) for p in ps]
    while n < CHUNK:
        parts = [_split(_split_dot(p, p)) for p in parts]
        ts = [t + _split_dot(_split(t), p) for t, p in zip(ts, parts)]
        n *= 2
    return ts


def _dn_mixer_kernel(qkv_ref, gate_ref, g_ref, conv_ref, gpar_ref, nw_ref,
                     o_ref, state_ref, tail_ref):
    c = pl.program_id(1)

    @pl.when(c == 0)
    def _():
        state_ref[...] = jnp.zeros_like(state_ref)
        tail_ref[...] = jnp.zeros_like(tail_ref)

    w = DN_HEADS * HEAD_DIM
    qkv = _causal_conv_silu(qkv_ref[...], tail_ref, conv_ref)
    gates = g_ref[...]
    beta_all = _sigmoid(gates)
    g_all = -jnp.exp(gpar_ref[1:2, :]) * _softplus(gates + gpar_ref[0:1, :])
    gcum = _dot(_tril(), g_all, precision=HIGHEST)
    gcum_t = gcum.T
    lower = _tril(None)
    strict = _tril(None, 1)
    heads = range(DN_HEADS)

    a_mats, locs = [], []
    for h in heads:
        q = qkv[:, h * HEAD_DIM:(h + 1) * HEAD_DIM]
        k = qkv[:, w + h * HEAD_DIM:w + (h + 1) * HEAD_DIM]
        v = qkv[:, 2 * w + h * HEAD_DIM:2 * w + (h + 1) * HEAD_DIM]
        q = q * lax.rsqrt(jnp.sum(q * q, axis=-1, keepdims=True) + NORM_EPS) * HEAD_DIM ** -0.5
        k = k * lax.rsqrt(jnp.sum(k * k, axis=-1, keepdims=True) + NORM_EPS)
        beta = beta_all[:, h:h + 1]
        g_col = gcum[:, DN_HEADS + h:DN_HEADS + h + 1]
        g_row = gcum_t[DN_HEADS + h:DN_HEADS + h + 1, :]
        g_last = g_col[CHUNK - 1:CHUNK, :]
        decay = jnp.exp(jnp.where(lower, g_col - g_row, NEG_INF))
        kb = k * beta
        kbf = k.astype(BF16)
        a_mats.append(jnp.where(strict, _dot_nt(kb.astype(BF16), kbf) * decay, 0.0))
        eg = jnp.exp(g_col)
        locs.append(dict(
            rhs=jnp.concatenate([v * beta, kb * eg], axis=1),
            attn=(_dot_nt(q.astype(BF16), kbf) * decay).astype(BF16),
            q_dec=(q * eg).astype(BF16),
            k_dec_t=(k * jnp.exp(g_last - g_col)).T.astype(BF16),
            g_last=jnp.exp(g_last)))

    t_invs = _unit_lower_inverses(a_mats)
    uws = [_split_dot(_split(t), _split(loc["rhs"])) for t, loc in zip(t_invs, locs)]

    for h, uw, loc in zip(heads, uws, locs):
        sl = slice(h * HEAD_DIM, (h + 1) * HEAD_DIM)
        u = uw[:, :HEAD_DIM]
        wm = uw[:, HEAD_DIM:]
        st = state_ref[h]
        stb = st.astype(BF16)
        vnb = (u - _dot(wm.astype(BF16), stb)).astype(BF16)
        o = _dot(loc["q_dec"], stb) + _dot(loc["attn"], vnb)
        state_ref[h] = loc["g_last"] * st + _dot(loc["k_dec_t"], vnb)
        y = _rms(o, nw_ref[...]) * _silu(gate_ref[:, sl])
        o_ref[:, sl] = y.astype(o_ref.dtype)


def _dn_mixer(z, B, S, conv_w, a_log, dt_bias, norm_w):
    N = z.shape[1]
    z3 = z.reshape(B, S, N)
    nc = S // CHUNK
    w = DN_HEADS * HEAD_DIM
    gpar = jnp.zeros((8, LANES), F32)
    gpar = gpar.at[0, DN_HEADS:2 * DN_HEADS].set(dt_bias).at[1, DN_HEADS:2 * DN_HEADS].set(a_log)

    def full(shape):
        return pl.BlockSpec(shape, lambda b, c: (0,) * len(shape))

    return pl.pallas_call(
        _dn_mixer_kernel,
        grid=(B, nc),
        in_specs=[pl.BlockSpec((None, CHUNK, 3 * w), lambda b, c: (b, c, 0)),
                  pl.BlockSpec((None, CHUNK, w), lambda b, c: (b, c, 3)),
                  pl.BlockSpec((None, CHUNK, LANES), lambda b, c: (b, c, 4 * w // LANES)),
                  full((CONV_K, 3 * w)), full((8, LANES)), full((1, HEAD_DIM))],
        out_specs=pl.BlockSpec((None, CHUNK, w), lambda b, c: (b, c, 0)),
        out_shape=jax.ShapeDtypeStruct((B, S, w), BF16),
        scratch_shapes=[pltpu.VMEM((DN_HEADS, HEAD_DIM, HEAD_DIM), F32),
                        pltpu.VMEM((8, 3 * w), F32)],
        compiler_params=_params("parallel", "arbitrary"),
        name="dn_mixer",
    )(z3, z3, z3, conv_w, gpar, norm_w.reshape(1, HEAD_DIM)).reshape(B * S, w)


def _pack_rows(x):
    return pltpu.bitcast(x.astype(BF16), jnp.uint32)


def _candidate_pairs(n):
    return [(i, j) for i in range(n) for j in range(n) if (i + 1) * (j + 1) <= n]


def _peer_route_kernel(h_ref, nw_ref, wq_ref, keys_ref, xn_ref, cnt_ref, e1_ref, qe_ref,
                       s_ref, top_ref, st_ref):
    tt = xn_ref.shape[1]
    xn = _rms(h_ref[...], nw_ref[...])
    xn_t = xn.T.astype(BF16)
    xn_ref[...] = xn_t
    q_t = _dot(wq_ref[...], xn_t)
    for h in range(PEER_HEADS):
        for p in range(2):
            lo = (2 * h + p) * PEER_NK
            s = _dot(keys_ref[p], q_t[lo:lo + PEER_NK].astype(BF16))
            s_ref[p, h] = s

    n_top = PEER_TOPK + 1
    top_ref[:, 0:1] = jnp.full((2, 1) + top_ref.shape[2:], float("inf"), F32)

    def extract(r, carry):
        for p in range(2):
            for h in range(PEER_HEADS):
                x = s_ref[p, h]
                prev = top_ref[p, pl.ds(r, 1), h:h + 1, :][0]
                m = jnp.max(jnp.where(x < prev, x, NEG_INF), axis=0, keepdims=True)
                top_ref[p, pl.ds(r + 1, 1), h:h + 1, :] = m[None]
        return carry

    lax.fori_loop(0, n_top, extract, 0)

    a = [top_ref[0, r + 1] for r in range(n_top)]
    b = [top_ref[1, r + 1] for r in range(n_top)]
    cands = [a[i] + b[j] for i, j in _candidate_pairs(n_top)]
    cur = list(cands)
    for _ in range(PEER_TOPK - 1):
        m = functools.reduce(jnp.maximum, cur)
        cur = [jnp.where(x >= m, NEG_INF, x) for x in cur]
    kth = functools.reduce(jnp.maximum, cur)
    cur = [jnp.where(x >= kth, NEG_INF, x) for x in cur]
    nxt = functools.reduce(jnp.maximum, cur)
    theta = jnp.where(nxt > NEG_INF, 0.5 * (kth + nxt), kth)
    top = a[0] + b[0]
    z = functools.reduce(lambda u, v: u + v,
                         [jnp.where(x >= theta, jnp.exp(x - top), 0.0) for x in cands])
    st_ref[0:8, :] = theta
    st_ref[8:16, :] = a[0]
    st_ref[16:24, :] = b[0]
    st_ref[24:32, :] = 0.5 / z

    groups = PEER_NK // BF16_ROWS
    for h in range(PEER_HEADS):
        s1 = s_ref[0, h]
        s2 = s_ref[1, h]
        tau = st_ref[h:h + 1, :] - s1
        cnt = jnp.zeros_like(s1)
        for r in range(PEER_TOPK):
            cnt = jnp.where(top_ref[1, r + 1, h:h + 1, :] >= tau, float(r + 1), cnt)
        cnt_ref[h] = cnt
        rank = jnp.full(s2.shape, float(PEER_TOPK), F32)
        for r in reversed(range(PEER_TOPK)):
            rank = jnp.where(s2 >= top_ref[1, r + 1, h:h + 1, :], float(r), rank)
        e1_ref[h] = jnp.exp(s1 - st_ref[8 + h:9 + h, :]) * st_ref[24 + h:25 + h, :]
        e2 = jnp.exp(s2 - st_ref[16 + h:17 + h, :])
        for c in range(tt // LANES):
            lanes = slice(c * LANES, (c + 1) * LANES)
            qe_ref[h, c, :, 0] = _pack_rows(rank[:, lanes]).reshape(groups, SUBLANES, LANES)
            qe_ref[h, c, :, 1] = _pack_rows(e2[:, lanes]).reshape(groups, SUBLANES, LANES)


def _peer_route(h, norm_w, wq_t, keys, tt):
    T, D = h.shape
    nt = T // tt
    nq = wq_t.shape[0]
    hs = (PEER_HEADS, PEER_NK, tt)
    qs = (PEER_HEADS, tt // LANES, PEER_NK // BF16_ROWS, 2, SUBLANES, LANES)
    return pl.pallas_call(
        _peer_route_kernel,
        grid=(nt,),
        in_specs=[pl.BlockSpec((tt, D), lambda i: (i, 0)),
                  pl.BlockSpec((1, D), lambda i: (0, 0)),
                  pl.BlockSpec((nq, D), lambda i: (0, 0)),
                  pl.BlockSpec((2, PEER_NK, PEER_NK), lambda i: (0, 0, 0))],
        out_specs=[pl.BlockSpec((None, D, tt), lambda i: (i, 0, 0)),
                   pl.BlockSpec((None,) + hs, lambda i: (i, 0, 0, 0)),
                   pl.BlockSpec((None,) + hs, lambda i: (i, 0, 0, 0)),
                   pl.BlockSpec((None,) + qs, lambda i: (i, 0, 0, 0, 0, 0, 0))],
        out_shape=[jax.ShapeDtypeStruct((nt, D, tt), BF16),
                   jax.ShapeDtypeStruct((nt,) + hs, F32),
                   jax.ShapeDtypeStruct((nt,) + hs, F32),
                   jax.ShapeDtypeStruct((nt,) + qs, jnp.uint32)],
        scratch_shapes=[pltpu.VMEM((2,) + hs, F32),
                        pltpu.VMEM((2, PEER_TOPK + 2, PEER_HEADS, tt), F32),
                        pltpu.VMEM((4 * PEER_HEADS, tt), F32)],
        compiler_params=_params("parallel"),
        name="peer_route",
    )(h, norm_w.reshape(1, D), wq_t, keys)


def _peer_dense_kernel(xn_ref, u_ref, vt_ref, cnt_ref, e1_ref, qe_ref, h_ref, o_ref,
                       acc_ref, act_ref, wg_ref):
    j = pl.program_id(1)
    eb, tt = wg_ref.shape
    nblk = eb // PEER_NK

    @pl.when(j == 0)
    def _():
        acc_ref[...] = jnp.zeros_like(acc_ref)

    act_ref[...] = _dot(u_ref[...], xn_ref[...])
    for kb in range(nblk):
        i1 = j * nblk + kb
        cnt_rows = [cnt_ref[h, pl.ds(i1, 1), :] for h in range(PEER_HEADS)]
        e1_rows = [e1_ref[h, pl.ds(i1, 1), :] for h in range(PEER_HEADS)]
        for c in range(tt // LANES):
            lanes = slice(c * LANES, (c + 1) * LANES)
            shape = (BF16_ROWS, LANES)
            cnts = [jnp.broadcast_to(r[:, lanes], shape).astype(BF16) for r in cnt_rows]
            e1s = [jnp.broadcast_to(r[:, lanes], shape).astype(BF16) for r in e1_rows]
            for g in range(PEER_NK // BF16_ROWS):
                w = None
                for h in range(PEER_HEADS):
                    rank = pltpu.bitcast(qe_ref[h, c, g, 0], BF16)
                    e2 = pltpu.bitcast(qe_ref[h, c, g, 1], BF16)
                    t = jnp.where(rank < cnts[h], e2, 0.0) * e1s[h]
                    w = t if w is None else w + t
                rows = slice(kb * PEER_NK + g * BF16_ROWS, kb * PEER_NK + (g + 1) * BF16_ROWS)
                a = act_ref[rows, lanes]
                gated = (w.astype(F32) * a) * (1.0 + lax.erf(a * (1.0 / math.sqrt(2.0))))
                wg_ref[rows, lanes] = gated.astype(BF16)
    acc_ref[...] += _dot(vt_ref[...], wg_ref[...])

    @pl.when(j == pl.num_programs(1) - 1)
    def _():
        o_ref[...] = h_ref[...] + acc_ref[...].T


def _peer_dense(xn_t, u, v_blocks, cnt, e1, qe, h):
    nt, D, tt = xn_t.shape
    nb, _, eb = v_blocks.shape
    hs = (PEER_HEADS, PEER_NK, tt)
    qs = qe.shape[1:]
    return pl.pallas_call(
        _peer_dense_kernel,
        grid=(nt, nb),
        in_specs=[pl.BlockSpec((None, D, tt), lambda i, j: (i, 0, 0)),
                  pl.BlockSpec((eb, D), lambda i, j: (j, 0)),
                  pl.BlockSpec((None, D, eb), lambda i, j: (j, 0, 0)),
                  pl.BlockSpec((None,) + hs, lambda i, j: (i, 0, 0, 0)),
                  pl.BlockSpec((None,) + hs, lambda i, j: (i, 0, 0, 0)),
                  pl.BlockSpec((None,) + qs, lambda i, j: (i, 0, 0, 0, 0, 0, 0)),
                  pl.BlockSpec((tt, D), lambda i, j: (i, 0))],
        out_specs=pl.BlockSpec((tt, D), lambda i, j: (i, 0)),
        out_shape=jax.ShapeDtypeStruct(h.shape, F32),
        scratch_shapes=[pltpu.VMEM((D, tt), F32),
                        pltpu.VMEM((eb, tt), F32),
                        pltpu.VMEM((eb, tt), BF16)],
        compiler_params=_params("parallel", "arbitrary"),
        name="peer_dense",
    )(xn_t, u, v_blocks, cnt, e1, qe, h)


def _ple_kernel(h_ref, p_ref, nw_ref, wg_ref, wp_ref, fw_ref, o_ref, *, final_norm):
    h = h_ref[...]
    gate = _sigmoid(_dot(_rms(h, nw_ref[...]).astype(BF16), wg_ref[...]))
    e = _dot(p_ref[...].astype(BF16), wp_ref[...])
    out = h + gate * e
    if final_norm:
        out = _rms(out, fw_ref[...])
    o_ref[...] = out


def _ple(h, p, norm_w, w_gate, w_proj, final_w, final_norm):
    T, D = h.shape
    P = p.shape[1]
    tm = _pick_tile(T, 512)
    return pl.pallas_call(
        functools.partial(_ple_kernel, final_norm=final_norm),
        grid=(T // tm,),
        in_specs=[pl.BlockSpec((tm, D), lambda i: (i, 0)),
                  pl.BlockSpec((tm, P), lambda i: (i, 0)),
                  pl.BlockSpec((1, D), lambda i: (0, 0)),
                  pl.BlockSpec((D, D), lambda i: (0, 0)),
                  pl.BlockSpec((P, D), lambda i: (0, 0)),
                  pl.BlockSpec((1, D), lambda i: (0, 0))],
        out_specs=pl.BlockSpec((tm, D), lambda i: (i, 0)),
        out_shape=jax.ShapeDtypeStruct((T, D), F32),
        compiler_params=_params("parallel"),
        name="ple",
    )(h, p, norm_w.reshape(1, D), w_gate, w_proj, final_w.reshape(1, D))


def _pad_cols(w, mult=LANES):
    n = w.shape[1]
    return jnp.pad(w, ((0, 0), (0, (-n) % mult)))


def kernel(x, p, norm_mix_w, norm_ffn_w, ab_w_in, ab_conv_w, ab_b_i, ab_b_f, ab_gn_ret, ab_gn_mlstm, ab_w_out, dn_w_in, dn_conv_w, dn_a_log, dn_dt_bias, dn_norm_w, dn_w_out, peer_w_q, peer_sub_keys, peer_u, peer_v, ple_w_proj, ple_w_gate, ple_norm_w, final_norm_w):
    B, S, D = x.shape
    depth = p.shape[0]
    T = B * S
    h = x.reshape(T, D)
    for i in range(depth):
        j = i // 2
        if i % 2 == 0:
            z = _norm_matmul(h, norm_mix_w[i], _pad_cols(ab_w_in[j]).astype(BF16))
            mixed = _ab_mixer(z, B, S, ab_conv_w[j], ab_b_i[j], ab_b_f[j], ab_gn_ret[j], ab_gn_mlstm[j])
            h = _matmul_residual(mixed, ab_w_out[j].astype(BF16), h)
        else:
            z = _norm_matmul(h, norm_mix_w[i], _pad_cols(dn_w_in[j]).astype(BF16))
            mixed = _dn_mixer(z, B, S, dn_conv_w[j], dn_a_log[j], dn_dt_bias[j], dn_norm_w[j])
            h = _matmul_residual(mixed, dn_w_out[j].astype(BF16), h)
        xn_t, cnt, e1, qe = _peer_route(h, norm_ffn_w[i], peer_w_q[i].T.astype(BF16),
                                        peer_sub_keys[i].astype(BF16), _pick_tile(T, PEER_TOKEN_TILE))
        n_exp = peer_v.shape[1]
        v_blocks = peer_v[i].astype(BF16).reshape(n_exp // PEER_EXPERT_BLOCK, PEER_EXPERT_BLOCK, D)
        h = _peer_dense(xn_t, peer_u[i].astype(BF16), v_blocks.transpose(0, 2, 1), cnt, e1, qe, h)
        h = _ple(h, p[i].reshape(T, -1), ple_norm_w[i], ple_w_gate[i].astype(BF16),
                 ple_w_proj[i].astype(BF16), final_norm_w, final_norm=(i == depth - 1))
    return h.reshape(B, S, D)
```
